```python
import jax, jax.numpy as jnp
from jax import lax
import numpy as np

D_MODEL = 1024
BATCH = 8
SEQ = 2048
DEPTH = 4

CHUNK = 64
D_MIX = 2 * D_MODEL

LRU_WIDTH = D_MIX // 4
LRU_BLOCKS = 8
LRU_BLOCK = LRU_WIDTH // LRU_BLOCKS
LRU_C = 8.0
CONV_W = 4

GLA_WIDTH = D_MIX // 4
GLA_HEADS = 4
GLA_DV = GLA_WIDTH // GLA_HEADS
GLA_DK = GLA_DV // 2
GLA_QK = GLA_HEADS * GLA_DK
GLA_RANK = 16
GLA_TAU = 16.0

SSD_WIDTH = D_MIX // 2
SSD_HEADDIM = 64
SSD_HEADS = SSD_WIDTH // SSD_HEADDIM
SSD_GROUPS = 2
SSD_HPG = SSD_HEADS // SSD_GROUPS
SSD_STATE = 128
SSD_CONV_DIM = SSD_WIDTH + 2 * SSD_GROUPS * SSD_STATE

N_EXPERTS = 64
TOP_K = 8
N_EXPERT_GROUPS = 8
EXPERTS_PER_GROUP = N_EXPERTS // N_EXPERT_GROUPS
TOPK_GROUPS = 4
D_EXPERT = 256
ROUTE_SCALE = 2.5

IN_SIZES = (LRU_WIDTH, LRU_WIDTH,
            GLA_QK, GLA_QK, GLA_WIDTH, GLA_WIDTH, GLA_RANK,
            SSD_WIDTH, SSD_CONV_DIM, SSD_HEADS)
D_IN_PROJ = sum(IN_SIZES)

DEEPNORM_ALPHA = (2 * DEPTH) ** 0.25
DEEPNORM_BETA = (8 * DEPTH) ** -0.25
LN_EPS = 1e-5
RMS_EPS = 1e-6

kernel_name = 'hybrid_lru_gla_ssd_moe_deepnorm_adaln'


def layer_norm(x, g, b):
    xf = x.astype(jnp.float32)
    mu = jnp.mean(xf, -1, keepdims=True)
    var = jnp.mean(jnp.square(xf - mu), -1, keepdims=True)
    return ((xf - mu) * lax.rsqrt(var + LN_EPS) * g + b).astype(x.dtype)


def rms_norm(x, g):
    xf = x.astype(jnp.float32)
    return (xf * lax.rsqrt(jnp.mean(jnp.square(xf), -1, keepdims=True) + RMS_EPS) * g).astype(x.dtype)


def causal_depthwise_conv(x, w, b):
    k, ch = w.shape
    y = lax.conv_general_dilated(x, w[:, None, :].astype(x.dtype), window_strides=(1,),
                                 padding=[(k - 1, 0)], dimension_numbers=('NWC', 'WIO', 'NWC'),
                                 feature_group_count=ch)
    return y + b.astype(x.dtype)


def _linear_recurrence_op(earlier, later):
    a1, b1 = earlier
    a2, b2 = later
    return a1 * a2, a2 * b1 + b2


def rglru_group(xb, gate, conv_w, conv_b, w_a, b_a, w_x, b_x, lam):
    bsz, seq, width = xb.shape
    xc = causal_depthwise_conv(xb, conv_w, conv_b)
    blocks = xc.reshape(bsz, seq, LRU_BLOCKS, LRU_BLOCK)
    r = jax.nn.sigmoid(jnp.einsum('bsni,nij->bsnj', blocks, w_a).reshape(bsz, seq, width) + b_a)
    i = jax.nn.sigmoid(jnp.einsum('bsni,nij->bsnj', blocks, w_x).reshape(bsz, seq, width) + b_x)
    log_a = -LRU_C * r.astype(jnp.float32) * jax.nn.softplus(-lam.astype(jnp.float32))
    a = jnp.exp(log_a)
    u = jnp.sqrt(-jnp.expm1(2.0 * log_a)) * (i * xc)
    _, h = lax.associative_scan(_linear_recurrence_op, (a, u), axis=1)
    return (h * jax.nn.gelu(gate)).astype(gate.dtype)


def gla_group(q, k, v, g, r, w_alpha, b_alpha, norm_g):
    bsz, seq, _ = q.shape
    n = seq // CHUNK

    def heads(t, d):
        return t.reshape(bsz, n, CHUNK, GLA_HEADS, d).transpose(0, 3, 1, 2, 4)

    log_alpha = jax.nn.log_sigmoid((r @ w_alpha + b_alpha).astype(jnp.float32)) / GLA_TAU
    q = heads(q, GLA_DK) * (GLA_DK ** -0.5)
    k = heads(k, GLA_DK)
    v = heads(v, GLA_DV)
    cum = jnp.cumsum(heads(log_alpha, GLA_DK), axis=3)
    q_dec = q * jnp.exp(cum)
    k_dec = k * jnp.exp(-cum)
    causal = jnp.tril(jnp.ones((CHUNK, CHUNK), dtype=bool))
    att = jnp.where(causal, jnp.einsum('bhnid,bhnjd->bhnij', q_dec, k_dec), 0.0)
    o_intra = jnp.einsum('bhnij,bhnjv->bhniv', att, v)
    cum_last = cum[:, :, :, -1]
    k_to_end = k * jnp.exp(cum_last[:, :, :, None, :] - cum)
    chunk_kv = jnp.einsum('bhncd,bhncv->bhndv', k_to_end, v)

    def step(state, inp):
        decay, kv = inp
        return decay[..., None] * state + kv, state

    _, s_prev = lax.scan(step, jnp.zeros_like(chunk_kv[:, :, 0]),
                         (jnp.moveaxis(jnp.exp(cum_last), 2, 0), jnp.moveaxis(chunk_kv, 2, 0)))
    o_inter = jnp.einsum('bhnid,bhndv->bhniv', q_dec, jnp.moveaxis(s_prev, 0, 2))
    o = rms_norm(o_intra + o_inter, norm_g)
    o = o.transpose(0, 2, 3, 1, 4).reshape(bsz, seq, GLA_WIDTH)
    return (o * jax.nn.silu(g)).astype(g.dtype)


def ssd_group(z, xbc, dt_raw, conv_w, conv_b, dt_bias, a_log, d_skip, norm_g):
    bsz, seq, _ = z.shape
    n = seq // CHUNK
    xbc = jax.nn.silu(causal_depthwise_conv(xbc, conv_w, conv_b))
    xs, bm, cm = jnp.split(xbc, [SSD_WIDTH, SSD_WIDTH + SSD_GROUPS * SSD_STATE], axis=-1)
    xs = xs.reshape(bsz, n, CHUNK, SSD_GROUPS, SSD_HPG, SSD_HEADDIM)
    bm = bm.reshape(bsz, n, CHUNK, SSD_GROUPS, SSD_STATE)
    cm = cm.reshape(bsz, n, CHUNK, SSD_GROUPS, SSD_STATE)
    dt = jax.nn.softplus((dt_raw + dt_bias).astype(jnp.float32)).reshape(bsz, n, CHUNK, SSD_GROUPS, SSD_HPG)
    a_neg = -jnp.exp(a_log.astype(jnp.float32)).reshape(SSD_GROUPS, SSD_HPG)
    a_cs = jnp.cumsum(dt * a_neg, axis=2)
    a_t = jnp.moveaxis(a_cs, 2, -1)
    causal = jnp.tril(jnp.ones((CHUNK, CHUNK), dtype=bool))
    decay_in = jnp.exp(jnp.where(causal, a_t[..., :, None] - a_t[..., None, :], -jnp.inf))
    cb = jnp.einsum('bnigs,bnjgs->bngij', cm, bm)
    xdt = xs * dt[..., None]
    y_diag = jnp.einsum('bnghij,bnjghp->bnighp', cb[:, :, :, None] * decay_in, xdt)
    decay_to_end = jnp.exp(a_cs[:, :, -1:] - a_cs)
    states = jnp.einsum('bnjgs,bnjghp->bnghps', bm, xdt * decay_to_end[..., None])

    def step(state, inp):
        log_decay, st = inp
        return jnp.exp(log_decay)[..., None, None] * state + st, state

    _, s_prev = lax.scan(step, jnp.zeros_like(states[:, 0]),
                         (jnp.moveaxis(a_cs[:, :, -1], 1, 0), jnp.moveaxis(states, 1, 0)))
    y_off = jnp.einsum('bnigs,bnghps->bnighp', cm, jnp.moveaxis(s_prev, 0, 1)) * jnp.exp(a_cs)[..., None]
    y = y_diag + y_off + xs * d_skip.reshape(SSD_GROUPS, SSD_HPG, 1)
    y = y.reshape(bsz, seq, SSD_WIDTH) * jax.nn.silu(z)
    y = rms_norm(y.reshape(bsz, seq, SSD_GROUPS, SSD_WIDTH // SSD_GROUPS),
                 norm_g.reshape(SSD_GROUPS, SSD_WIDTH // SSD_GROUPS)).reshape(bsz, seq, SSD_WIDTH)
    return y.astype(z.dtype)


def token_mixer(h, w_in, lru_conv_w, lru_conv_b, lru_w_a, lru_b_a, lru_w_x, lru_b_x, lru_lambda,
                gla_w_alpha, gla_b_alpha, gla_norm, ssd_conv_w, ssd_conv_b, ssd_dt_bias, ssd_a_log,
                ssd_d, ssd_norm, w_out):
    proj = h @ w_in
    splits = np.cumsum(IN_SIZES)[:-1].tolist()
    (lru_x, lru_gate, gla_q, gla_k, gla_v, gla_g, gla_r,
     ssd_z, ssd_xbc, ssd_dt) = jnp.split(proj, splits, axis=-1)
    y_lru = rglru_group(lru_x, lru_gate, lru_conv_w, lru_conv_b, lru_w_a, lru_b_a, lru_w_x, lru_b_x, lru_lambda)
    y_gla = gla_group(gla_q, gla_k, gla_v, gla_g, gla_r, gla_w_alpha, gla_b_alpha, gla_norm)
    y_ssd = ssd_group(ssd_z, ssd_xbc, ssd_dt, ssd_conv_w, ssd_conv_b, ssd_dt_bias, ssd_a_log, ssd_d, ssd_norm)
    return jnp.concatenate([y_lru, y_gla, y_ssd], axis=-1) @ w_out


def swiglu(t, w1, w3, w2):
    return (jax.nn.silu(t @ w1) * (t @ w3)) @ w2


def moe_ffn(h, router_w, router_bias, w1, w3, w2, sw1, sw3, sw2):
    bsz, seq, d = h.shape
    t = h.reshape(bsz * seq, d)
    n_tok = t.shape[0]
    rows = jnp.arange(n_tok)[:, None]
    scores = jax.nn.sigmoid((t @ router_w).astype(jnp.float32))
    biased = scores + router_bias.astype(jnp.float32)
    grouped = biased.reshape(n_tok, N_EXPERT_GROUPS, EXPERTS_PER_GROUP)
    group_score = lax.top_k(grouped, 2)[0].sum(-1)
    _, top_groups = lax.top_k(group_score, TOPK_GROUPS)
    group_mask = jnp.zeros((n_tok, N_EXPERT_GROUPS), dtype=bool).at[rows, top_groups].set(True)
    allowed = jnp.repeat(group_mask, EXPERTS_PER_GROUP, axis=1)
    _, top_idx = lax.top_k(jnp.where(allowed, biased, -jnp.inf), TOP_K)
    w = jnp.take_along_axis(scores, top_idx, axis=1)
    w = ROUTE_SCALE * w / jnp.sum(w, -1, keepdims=True)
    gates = jnp.zeros((n_tok, N_EXPERTS), jnp.float32).at[rows, top_idx].set(w).astype(t.dtype)
    y = swiglu(t, sw1, sw3, sw2)
    for gi in range(N_EXPERT_GROUPS):
        sl = slice(gi * EXPERTS_PER_GROUP, (gi + 1) * EXPERTS_PER_GROUP)
        hid = jax.nn.silu(jnp.einsum('td,edf->tef', t, w1[sl])) * jnp.einsum('td,edf->tef', t, w3[sl])
        y = y + jnp.einsum('tef,efd->td', hid * gates[:, sl, None], w2[sl])
    return y.reshape(bsz, seq, d)


def setup_inputs(seed: int = 0) -> dict:
    key = jax.random.key(seed)
    ks = iter(jax.random.split(key, 48))

    def nrm(shape, scale):
        return jax.random.normal(next(ks), shape, jnp.float32) * scale

    def unif(shape, lo, hi):
        return jax.random.uniform(next(ks), shape, jnp.float32, minval=lo, maxval=hi)

    L, D = DEPTH, D_MODEL
    x = nrm((BATCH, SEQ, D), 1.0)
    c = nrm((BATCH, D), 1.0)
    w_ada = nrm((L, D, 6 * D), 0.5 * D ** -0.5)
    b_ada = nrm((L, 6 * D), 0.01)
    w_in = nrm((L, D, D_IN_PROJ), D ** -0.5)
    lru_conv_w = nrm((L, CONV_W, LRU_WIDTH), CONV_W ** -0.5)
    lru_conv_b = nrm((L, LRU_WIDTH), 0.01)
    lru_w_a = nrm((L, LRU_BLOCKS, LRU_BLOCK, LRU_BLOCK), LRU_BLOCK ** -0.5)
    lru_b_a = nrm((L, LRU_WIDTH), 0.01)
    lru_w_x = nrm((L, LRU_BLOCKS, LRU_BLOCK, LRU_BLOCK), LRU_BLOCK ** -0.5)
    lru_b_x = nrm((L, LRU_WIDTH), 0.01)
    p = unif((L, LRU_WIDTH), 0.9, 0.999) ** (1.0 / LRU_C)
    lru_lambda = jnp.log(p) - jnp.log1p(-p)
    gla_w_alpha = nrm((L, GLA_RANK, GLA_QK), GLA_RANK ** -0.5)
    gla_b_alpha = nrm((L, GLA_QK), 0.1)
    gla_norm = 1.0 + nrm((L, GLA_DV), 0.01)
    ssd_conv_w = nrm((L, CONV_W, SSD_CONV_DIM), CONV_W ** -0.5)
    ssd_conv_b = nrm((L, SSD_CONV_DIM), 0.01)
    dt0 = jnp.exp(unif((L, SSD_HEADS), float(np.log(1e-3)), float(np.log(1e-1))))
    ssd_dt_bias = dt0 + jnp.log(-jnp.expm1(-dt0))
    ssd_a_log = jnp.log(unif((L, SSD_HEADS), 1.0, 16.0))
    ssd_d = 1.0 + nrm((L, SSD_HEADS), 0.01)
    ssd_norm = 1.0 + nrm((L, SSD_WIDTH), 0.01)
    w_out = nrm((L, D_MIX, D), DEEPNORM_BETA * D_MIX ** -0.5)
    ln1_g = 1.0 + nrm((L, D), 0.01)
    ln1_b = nrm((L, D), 0.01)
    router_w = nrm((L, D, N_EXPERTS), D ** -0.5)
    router_bias = nrm((L, N_EXPERTS), 0.01)
    exp_w1 = nrm((L, N_EXPERTS, D, D_EXPERT), D ** -0.5)
    exp_w3 = nrm((L, N_EXPERTS, D, D_EXPERT), D ** -0.5)
    exp_w2 = nrm((L, N_EXPERTS, D_EXPERT, D), DEEPNORM_BETA * D_EXPERT ** -0.5)
    shared_w1 = nrm((L, D, D_EXPERT), D ** -0.5)
    shared_w3 = nrm((L, D, D_EXPERT), D ** -0.5)
    shared_w2 = nrm((L, D_EXPERT, D), DEEPNORM_BETA * D_EXPERT ** -0.5)
    ln2_g = 1.0 + nrm((L, D), 0.01)
    ln2_b = nrm((L, D), 0.01)
    return {'x': x, 'c': c, 'w_ada': w_ada, 'b_ada': b_ada, 'w_in': w_in,
            'lru_conv_w': lru_conv_w, 'lru_conv_b': lru_conv_b, 'lru_w_a': lru_w_a, 'lru_b_a': lru_b_a,
            'lru_w_x': lru_w_x, 'lru_b_x': lru_b_x, 'lru_lambda': lru_lambda,
            'gla_w_alpha': gla_w_alpha, 'gla_b_alpha': gla_b_alpha, 'gla_norm': gla_norm,
            'ssd_conv_w': ssd_conv_w, 'ssd_conv_b': ssd_conv_b, 'ssd_dt_bias': ssd_dt_bias,
            'ssd_a_log': ssd_a_log, 'ssd_d': ssd_d, 'ssd_norm': ssd_norm, 'w_out': w_out,
            'ln1_g': ln1_g, 'ln1_b': ln1_b, 'router_w': router_w, 'router_bias': router_bias,
            'exp_w1': exp_w1, 'exp_w3': exp_w3, 'exp_w2': exp_w2,
            'shared_w1': shared_w1, 'shared_w3': shared_w3, 'shared_w2': shared_w2,
            'ln2_g': ln2_g, 'ln2_b': ln2_b}


def reference(x, c, w_ada, b_ada, w_in, lru_conv_w, lru_conv_b, lru_w_a, lru_b_a, lru_w_x, lru_b_x,
              lru_lambda, gla_w_alpha, gla_b_alpha, gla_norm, ssd_conv_w, ssd_conv_b, ssd_dt_bias,
              ssd_a_log, ssd_d, ssd_norm, w_out, ln1_g, ln1_b, router_w, router_bias,
              exp_w1, exp_w3, exp_w2, shared_w1, shared_w3, shared_w2, ln2_g, ln2_b):
    cond = jax.nn.silu(c)
    for l in range(DEPTH):
        mod = cond @ w_ada[l] + b_ada[l]
        sh1, sc1, g1, sh2, sc2, g2 = jnp.split(mod[:, None, :], 6, axis=-1)
        h = x * (1.0 + sc1) + sh1
        mix = token_mixer(h, w_in[l], lru_conv_w[l], lru_conv_b[l], lru_w_a[l], lru_b_a[l], lru_w_x[l],
                          lru_b_x[l], lru_lambda[l], gla_w_alpha[l], gla_b_alpha[l], gla_norm[l],
                          ssd_conv_w[l], ssd_conv_b[l], ssd_dt_bias[l], ssd_a_log[l], ssd_d[l],
                          ssd_norm[l], w_out[l])
        x = layer_norm(DEEPNORM_ALPHA * x + g1 * mix, ln1_g[l], ln1_b[l])
        h = x * (1.0 + sc2) + sh2
        ffn = moe_ffn(h, router_w[l], router_bias[l], exp_w1[l], exp_w3[l], exp_w2[l],
                      shared_w1[l], shared_w3[l], shared_w2[l])
        x = layer_norm(DEEPNORM_ALPHA * x + g2 * ffn, ln2_g[l], ln2_b[l])
    return x
```

```python
import functools

import jax
import jax.numpy as jnp
import numpy as np
from jax import lax
from jax.experimental import pallas as pl
from jax.experimental.pallas import tpu as pltpu

F32 = jnp.float32
BF16 = jnp.bfloat16

D_MODEL = 1024
CHUNK = 64
LRU_WIDTH = 512
LRU_BLOCKS = 8
LRU_C = 8.0
CONV_W = 4
GLA_WIDTH = 512
GLA_HEADS = 4
GLA_DV = 128
GLA_DK = 64
GLA_QK = 256
GLA_RANK = 16
GLA_TAU = 16.0
SSD_WIDTH = 1024
SSD_HEADDIM = 64
SSD_HEADS = 16
SSD_GROUPS = 2
SSD_HPG = 8
SSD_STATE = 128
SSD_CONV_DIM = SSD_WIDTH + 2 * SSD_GROUPS * SSD_STATE
D_MIX = 2048
N_EXPERTS = 64
TOP_K = 8
N_EXPERT_GROUPS = 8
EXPERTS_PER_GROUP = 8
TOPK_GROUPS = 4
D_EXPERT = 256
ROUTE_SCALE = 2.5
LN_EPS = 1e-5
RMS_EPS = 1e-6

C_LRU_X = 0
C_LRU_G = 512
C_Q = 1024
C_K = 1280
C_V = 1536
C_G = 2048
C_Z = 2560
C_XBC = 3584
C_RD = 5120
D_IN_PAD = 5248
RD_DT0 = 16

TS = 256
TB = 256
ROW_CHUNK = 16
CPT = 32
TILE_M = ROW_CHUNK * CPT
R_BLK = -(-(TOP_K * TB + N_EXPERTS * (ROW_CHUNK - 1)) // 256) * 256
N_GRP = R_BLK // 256
VMEM_LIMIT = 56 * 1024 * 1024


def _dot(a, b):
    return jnp.dot(a, b, preferred_element_type=F32)


def _dot_nt(a, b):
    return lax.dot_general(a, b, (((1,), (1,)), ((), ())), preferred_element_type=F32)


def _dot_tn(a, b):
    return lax.dot_general(a, b, (((0,), (0,)), ((), ())), preferred_element_type=F32)


def _sigmoid(x):
    return jax.nn.sigmoid(x)


def _silu(x):
    return x * _sigmoid(x)


def _softplus(x):
    return jnp.maximum(x, 0.0) + jnp.log1p(jnp.exp(-jnp.abs(x)))


def _gelu_tanh(x):
    return 0.5 * x * (1.0 + jnp.tanh(0.7978845608028654 * (x + 0.044715 * (x * x * x))))


def _layer_norm(x, g, b):
    mu = jnp.mean(x, axis=-1, keepdims=True)
    xc = x - mu
    var = jnp.mean(xc * xc, axis=-1, keepdims=True)
    return xc * lax.rsqrt(var + LN_EPS) * g + b


def _shift_rows(x, d, fill, rowmod):
    return jnp.where(rowmod >= d, pltpu.roll(x, d, axis=0), fill)


def _cumsum_rows(x, period):
    rowmod = lax.broadcasted_iota(jnp.int32, x.shape, 0) & (period - 1)
    d = 1
    while d < period:
        x = x + _shift_rows(x, d, 0.0, rowmod)
        d *= 2
    return x


def _causal_conv(stage_ref, x, w_ref, b_ref, first):
    n = x.shape[0]

    @pl.when(first)
    def _():
        stage_ref[0:8, :] = jnp.zeros((8, x.shape[1]), F32)

    stage_ref[8:8 + n, :] = x
    y = b_ref[...] + w_ref[CONV_W - 1:CONV_W, :] * x
    for k in range(CONV_W - 1):
        y = y + w_ref[k:k + 1, :] * stage_ref[5 + k:5 + k + n, :]
    stage_ref[0:8, :] = stage_ref[n:n + 8, :]
    return y


def _adaln_kernel(c_ref, w_ref, b_ref, o_ref):
    cond = _silu(c_ref[...])
    o_ref[...] = jnp.dot(cond, w_ref[...], preferred_element_type=F32,
                         precision=lax.Precision.HIGHEST) + b_ref[...]


def _adaln(c, w_ada, b_ada):
    n_layers, d, n = w_ada.shape
    bsz = c.shape[0]
    tn = 1024
    return pl.pallas_call(
        _adaln_kernel,
        grid=(n_layers, n // tn),
        in_specs=[pl.BlockSpec((bsz, d), lambda l, j: (0, 0)),
                  pl.BlockSpec((None, d, tn), lambda l, j: (l, 0, j)),
                  pl.BlockSpec((None, 1, tn), lambda l, j: (l, 0, j))],
        out_specs=pl.BlockSpec((None, bsz, tn), lambda l, j: (l, 0, j)),
        out_shape=jax.ShapeDtypeStruct((n_layers, bsz, n), F32),
        name="adaln",
    )(c, w_ada, b_ada.reshape(n_layers, 1, n))


def _mixer_kernel(x_ref, mod_ref, win_ref, lcw_ref, lcb_ref, wg_ref, bg_ref, lam_ref,
                  wal_ref, bal_ref, gnorm_ref, scw_ref, scb_ref, dtb_ref, alog_ref,
                  dskip_ref, snorm_ref, e_ref, wout_ref, lng_ref, lnb_ref,
                  o_ref,
                  stage_l, stage_s, hcar, gla_st, ssd_st, o_scr, cat, *, alpha):
    first = pl.program_id(1) == 0
    x = x_ref[...]
    sh1 = mod_ref[0:1, :]
    sc1 = mod_ref[1:2, :]
    g1 = mod_ref[2:3, :]
    h = (x * (1.0 + sc1) + sh1).astype(BF16)

    @pl.when(first)
    def _():
        hcar[...] = jnp.zeros(hcar.shape, F32)
        gla_st[...] = jnp.zeros(gla_st.shape, F32)
        ssd_st[...] = jnp.zeros(ssd_st.shape, F32)

    row = lax.broadcasted_iota(jnp.int32, (TS, 1), 0)

    lx = _dot(h, win_ref[:, C_LRU_X:C_LRU_X + LRU_WIDTH])
    xc = _causal_conv(stage_l, lx, lcw_ref, lcb_ref, first)
    gts = _dot(xc.astype(BF16), wg_ref[...]) + bg_ref[...]
    r = _sigmoid(gts[:, :LRU_WIDTH])
    i_g = _sigmoid(gts[:, LRU_WIDTH:])
    log_a = (-LRU_C) * r * _softplus(-lam_ref[...])
    a = jnp.exp(log_a)
    u = jnp.sqrt(-jnp.tanh(log_a) * (a * a + 1.0)) * (i_g * xc)
    acc_a, acc_h = a, u
    d = 1
    while d < TS:
        sh_h = jnp.where(row >= d, pltpu.roll(acc_h, d, axis=0), 0.0)
        sh_a = jnp.where(row >= d, pltpu.roll(acc_a, d, axis=0), 1.0)
        acc_h = acc_h + acc_a * sh_h
        acc_a = acc_a * sh_a
        d *= 2
    h_lru = acc_h + acc_a * hcar[0:1, :]
    hcar[0:1, :] = h_lru[TS - 1:TS, :]
    lgate = _dot(h, win_ref[:, C_LRU_G:C_LRU_G + LRU_WIDTH])
    cat[:, 0:LRU_WIDTH] = (h_lru * _gelu_tanh(lgate)).astype(BF16)

    rd = _dot(h, win_ref[:, C_RD:C_RD + 128])

    q = _dot(h, win_ref[:, C_Q:C_Q + GLA_QK]) * (GLA_DK ** -0.5)
    k = _dot(h, win_ref[:, C_K:C_K + GLA_QK])
    v = _dot(h, win_ref[:, C_V:C_V + GLA_WIDTH]).astype(BF16)
    zal = _dot(rd.astype(BF16), wal_ref[...]) + bal_ref[...]
    log_alpha = (jnp.minimum(zal, 0.0) - jnp.log1p(jnp.exp(-jnp.abs(zal)))) * (1.0 / GLA_TAU)
    cum = _cumsum_rows(log_alpha, CHUNK)
    q_dec = q * jnp.exp(cum)
    k_dec = (k * jnp.exp(-cum)).astype(BF16)
    ri = lax.broadcasted_iota(jnp.int32, (TS, TS), 0)
    ci = lax.broadcasted_iota(jnp.int32, (TS, TS), 1)
    causal = ri >= ci
    blk_causal = causal & ((ri // CHUNK) == (ci // CHUNK))
    lane_head = lax.broadcasted_iota(jnp.int32, (1, GLA_QK), 1) // GLA_DK
    for hd in range(GLA_HEADS):
        qm = jnp.where(lane_head == hd, q_dec, 0.0).astype(BF16)
        att = jnp.where(blk_causal, _dot_nt(qm, k_dec), 0.0).astype(BF16)
        o_scr[:, hd * GLA_DV:(hd + 1) * GLA_DV] = _dot(att, v[:, hd * GLA_DV:(hd + 1) * GLA_DV])
    sr = lax.broadcasted_iota(jnp.int32, (GLA_WIDTH, GLA_QK), 0) // GLA_DV
    scol = lax.broadcasted_iota(jnp.int32, (GLA_WIDTH, GLA_QK), 1) // GLA_DK
    st_mask = sr == scol
    q_dec_b = q_dec.astype(BF16)
    for c in range(TS // CHUNK):
        rows = slice(c * CHUNK, (c + 1) * CHUNK)
        st = gla_st[...]
        o_scr[rows, :] = o_scr[rows, :] + _dot_nt(q_dec_b[rows, :], st.astype(BF16))
        cum_c = cum[rows, :]
        cum_last = cum_c[CHUNK - 1:CHUNK, :]
        k_end = (k[rows, :] * jnp.exp(cum_last - cum_c)).astype(BF16)
        upd = _dot_tn(v[rows, :], k_end)
        gla_st[...] = st * jnp.exp(cum_last) + jnp.where(st_mask, upd, 0.0)
    g_gate = _dot(h, win_ref[:, C_G:C_G + GLA_WIDTH])
    for hd in range(GLA_HEADS):
        cols = slice(hd * GLA_DV, (hd + 1) * GLA_DV)
        oh = o_scr[:, cols]
        ms = jnp.mean(oh * oh, axis=-1, keepdims=True)
        on = oh * lax.rsqrt(ms + RMS_EPS) * gnorm_ref[:, cols]
        cat[:, LRU_WIDTH + hd * GLA_DV:LRU_WIDTH + (hd + 1) * GLA_DV] = (
            on * _silu(g_gate[:, cols])).astype(BF16)

    xbc = _dot(h, win_ref[:, C_XBC:C_XBC + SSD_CONV_DIM])
    xbc = _silu(_causal_conv(stage_s, xbc, scw_ref, scb_ref, first))
    xs = xbc[:, :SSD_WIDTH]
    lane128 = lax.broadcasted_iota(jnp.int32, (1, 128), 1)
    dt_lane = (lane128 >= RD_DT0) & (lane128 < RD_DT0 + SSD_HEADS)
    a_neg = jnp.where(dt_lane, -jnp.exp(alog_ref[...]), 0.0)
    dtp = _softplus(rd + dtb_ref[...])
    a_cs = _cumsum_rows(dtp * a_neg, TS)
    a_last = a_cs[TS - 1:TS, :]
    ea = jnp.exp(a_cs)
    dte = jnp.exp(a_last - a_cs)
    expand = _dot(jnp.concatenate([dtp, dte, ea], axis=0).astype(BF16), e_ref[...])
    dt_x = expand[0:TS, :]
    dte_x = expand[TS:2 * TS, :]
    ea_x = expand[2 * TS:3 * TS, :]
    xdt = xs * dt_x
    xdt_b = xdt.astype(BF16)
    xdte_b = (xdt * dte_x).astype(BF16)
    a_cs_t = a_cs.T
    z = _dot(h, win_ref[:, C_Z:C_Z + SSD_WIDTH])
    lane_lo = lax.broadcasted_iota(jnp.int32, (1, 128), 1) < SSD_HEADDIM
    gw = SSD_HPG * SSD_HEADDIM
    for g in range(SSD_GROUPS):
        bm = xbc[:, SSD_WIDTH + g * SSD_STATE:SSD_WIDTH + (g + 1) * SSD_STATE].astype(BF16)
        cm = xbc[:, SSD_WIDTH + (SSD_GROUPS + g) * SSD_STATE:
                 SSD_WIDTH + (SSD_GROUPS + g + 1) * SSD_STATE].astype(BF16)
        cb = _dot_nt(cm, bm)
        st = ssd_st[g]
        y_off = _dot(cm, st.astype(BF16)) * ea_x[:, g * gw:(g + 1) * gw]
        yg = []
        for j in range(SSD_HPG // 2):
            ys = []
            for hh in (2 * j, 2 * j + 1):
                lane = RD_DT0 + g * SSD_HPG + hh
                diff = a_cs[:, lane:lane + 1] - a_cs_t[lane:lane + 1, :]
                m = (cb * jnp.exp(jnp.where(causal, diff, -jnp.inf))).astype(BF16)
                ys.append(_dot(m, xdt_b[:, g * gw + j * 128:g * gw + (j + 1) * 128]))
            cols = slice(g * gw + j * 128, g * gw + (j + 1) * 128)
            y = jnp.where(lane_lo, ys[0], ys[1]) + y_off[:, j * 128:(j + 1) * 128]
            y = y + xs[:, cols] * dskip_ref[:, cols]
            yg.append(y * _silu(z[:, cols]))
        ssd_st[g] = st * ea_x[TS - 1:TS, g * gw:(g + 1) * gw] + _dot_tn(bm, xdte_b[:, g * gw:(g + 1) * gw])
        ssq = yg[0] * yg[0]
        for t in yg[1:]:
            ssq = ssq + t * t
        ms = jnp.sum(ssq, axis=-1, keepdims=True) * (1.0 / gw)
        inv = lax.rsqrt(ms + RMS_EPS)
        for j, t in enumerate(yg):
            cols = slice(g * gw + j * 128, g * gw + (j + 1) * 128)
            cat[:, LRU_WIDTH + GLA_WIDTH + g * gw + j * 128:
                LRU_WIDTH + GLA_WIDTH + g * gw + (j + 1) * 128] = (
                    t * inv * snorm_ref[:, cols]).astype(BF16)

    mix = _dot(cat[...], wout_ref[...])
    o_ref[...] = _layer_norm(alpha * x + g1 * mix, lng_ref[...], lnb_ref[...])


def _full(shape):
    nd = len(shape)
    return pl.BlockSpec(shape, lambda *_: (0,) * nd)


def _mixer(x, mod_l, p, alpha):
    bsz, seq, d = x.shape
    small = [p["lcw"], p["lcb"], p["wg"], p["bg"], p["lam"], p["wal"], p["bal"], p["gnorm"],
             p["scw"], p["scb"], p["dtb"], p["alog"], p["dskip"], p["snorm"], p["e"],
             p["wout"], p["lng"], p["lnb"]]
    in_specs = [pl.BlockSpec((None, TS, d), lambda b, s: (b, s, 0)),
                pl.BlockSpec((None, 6, d), lambda b, s: (b, 0, 0)),
                _full(p["win"].shape)] + [_full(a.shape) for a in small]
    return pl.pallas_call(
        functools.partial(_mixer_kernel, alpha=alpha),
        grid=(bsz, seq // TS),
        in_specs=in_specs,
        out_specs=pl.BlockSpec((None, TS, d), lambda b, s: (b, s, 0)),
        out_shape=jax.ShapeDtypeStruct((bsz, seq, d), F32),
        scratch_shapes=[pltpu.VMEM((TS + 8, LRU_WIDTH), F32),
                        pltpu.VMEM((TS + 8, SSD_CONV_DIM), F32),
                        pltpu.VMEM((8, LRU_WIDTH), F32),
                        pltpu.VMEM((GLA_WIDTH, GLA_QK), F32),
                        pltpu.VMEM((SSD_GROUPS, SSD_STATE, SSD_HPG * SSD_HEADDIM), F32),
                        pltpu.VMEM((TS, GLA_WIDTH), F32),
                        pltpu.VMEM((TS, D_MIX), BF16)],
        compiler_params=pltpu.CompilerParams(
            dimension_semantics=("arbitrary", "arbitrary"), vmem_limit_bytes=VMEM_LIMIT),
        name="mixer",
    )(x, mod_l, p["win"], *small)


def _router_kernel(x_ref, mod_ref, rwt_ref, rbias_ref, ustrict_ref, lstrict_ref,
                   xs_ref, pg_ref, nch_ref, msk_scr, pg_scr):
    x = x_ref[...]
    sh2 = mod_ref[3:4, :]
    sc2 = mod_ref[4:5, :]
    h2 = x * (1.0 + sc2) + sh2
    logits = lax.dot_general(rwt_ref[...], h2, (((1,), (1,)), ((), ())),
                             preferred_element_type=F32, precision=lax.Precision.HIGHEST)
    scores = _sigmoid(logits)
    biased = scores + rbias_ref[...]

    row8 = lax.broadcasted_iota(jnp.int32, (EXPERTS_PER_GROUP, TB), 0)
    gsc = []
    for g in range(N_EXPERT_GROUPS):
        vals = biased[g * EXPERTS_PER_GROUP:(g + 1) * EXPERTS_PER_GROUP, :]
        m1 = jnp.max(vals, axis=0, keepdims=True)
        i1 = jnp.min(jnp.where(vals == m1, row8, EXPERTS_PER_GROUP), axis=0, keepdims=True)
        m2 = jnp.max(jnp.where(row8 == i1, -jnp.inf, vals), axis=0, keepdims=True)
        gsc.append(m1 + m2)
    for g in range(N_EXPERT_GROUPS):
        beaten = jnp.zeros((1, TB), F32)
        for o in range(N_EXPERT_GROUPS):
            if o == g:
                continue
            wins = (gsc[o] >= gsc[g]) if o < g else (gsc[o] > gsc[g])
            beaten = beaten + jnp.where(wins, 1.0, 0.0)
        keep = beaten < TOPK_GROUPS
        vals = biased[g * EXPERTS_PER_GROUP:(g + 1) * EXPERTS_PER_GROUP, :]
        msk_scr[g * EXPERTS_PER_GROUP:(g + 1) * EXPERTS_PER_GROUP, :] = jnp.where(keep, vals, -jnp.inf)

    cur = msk_scr[...]
    row64 = lax.broadcasted_iota(jnp.int32, (N_EXPERTS, TB), 0)
    sel = jnp.zeros((N_EXPERTS, TB), F32)
    picks = []
    for _ in range(TOP_K):
        m = jnp.max(cur, axis=0, keepdims=True)
        ik = jnp.min(jnp.where(cur == m, row64, N_EXPERTS), axis=0, keepdims=True)
        hit = row64 == ik
        sel = jnp.where(hit, 1.0, sel)
        cur = jnp.where(hit, -jnp.inf, cur)
        picks.append(hit)
    wsel = sel * scores
    gates = wsel * (ROUTE_SCALE / jnp.sum(wsel, axis=0, keepdims=True))

    rank = _dot(sel.astype(BF16), ustrict_ref[...])
    cnt = jnp.sum(sel, axis=1, keepdims=True)
    nch = jnp.floor((cnt + (ROW_CHUNK - 1.0)) * (1.0 / ROW_CHUNK))
    nch_b = jnp.broadcast_to(nch, (N_EXPERTS, TB))
    choff = _dot(lstrict_ref[...], nch_b.astype(BF16))
    pos = choff * float(ROW_CHUNK) + rank
    nch_ref[...] = jnp.broadcast_to(nch, (N_EXPERTS, 128))

    pg_scr[...] = jnp.zeros(pg_scr.shape, F32)
    pos_k = []
    for kk, hit in enumerate(picks):
        pk = jnp.sum(jnp.where(hit, pos, 0.0), axis=0, keepdims=True)
        gk = jnp.sum(jnp.where(hit, gates, 0.0), axis=0, keepdims=True)
        pos_k.append(pk)
        pg_scr[kk:kk + 1, :] = pk
        pg_scr[TOP_K + kk:TOP_K + kk + 1, :] = gk
    pg_ref[...] = pg_scr[...].T

    h2b = h2.astype(BF16)
    riota = lax.broadcasted_iota(jnp.int32, (256, TB), 0).astype(F32)
    for grp in range(N_GRP):
        rr = riota + float(grp * 256)
        onehot = jnp.zeros((256, TB), F32)
        for pk in pos_k:
            onehot = jnp.where(pk == rr, 1.0, onehot)
        xs_ref[grp * 256:(grp + 1) * 256, :] = _dot(onehot.astype(BF16), h2b).astype(BF16)


def _router(x1, mod_l, rwt, rbias, ustrict, lstrict, blocks_per_batch):
    t, d = x1.shape
    nb = t // TB
    return pl.pallas_call(
        _router_kernel,
        grid=(nb,),
        in_specs=[pl.BlockSpec((TB, d), lambda i: (i, 0)),
                  pl.BlockSpec((None, 6, d), lambda i: (i // blocks_per_batch, 0, 0)),
                  _full(rwt.shape), _full(rbias.shape), _full(ustrict.shape), _full(lstrict.shape)],
        out_specs=[pl.BlockSpec((R_BLK, d), lambda i: (i, 0)),
                   pl.BlockSpec((None, TB, 128), lambda i: (i, 0, 0)),
                   pl.BlockSpec((None, N_EXPERTS, 128), lambda i: (i, 0, 0))],
        out_shape=[jax.ShapeDtypeStruct((nb * R_BLK, d), BF16),
                   jax.ShapeDtypeStruct((nb, TB, 128), F32),
                   jax.ShapeDtypeStruct((nb, N_EXPERTS, 128), F32)],
        scratch_shapes=[pltpu.VMEM((N_EXPERTS, TB), F32), pltpu.VMEM((128, TB), F32)],
        compiler_params=pltpu.CompilerParams(
            dimension_semantics=("arbitrary",), vmem_limit_bytes=VMEM_LIMIT),
        name="router",
    )(x1, mod_l, rwt, rbias, ustrict, lstrict)


def _expert_kernel(te_ref, src_ref, nt_ref, xs_hbm, w1_ref, w3_ref, w2_ref, ys_hbm,
                   xbuf, ybuf, w1b, w3b, w2b, sem_in, sem_out):
    j = pl.program_id(0)
    nt = nt_ref[0]
    slot = lax.rem(j, 2)

    def chunk_rows(cid):
        return pl.ds(pl.multiple_of(cid * ROW_CHUNK, ROW_CHUNK), ROW_CHUNK)

    def in_copy(cid, sl, i):
        return pltpu.make_async_copy(xs_hbm.at[chunk_rows(cid), :],
                                     xbuf.at[sl, pl.ds(i * ROW_CHUNK, ROW_CHUNK), :],
                                     sem_in.at[sl])

    def out_copy(cid, sl, i):
        return pltpu.make_async_copy(ybuf.at[sl, pl.ds(i * ROW_CHUNK, ROW_CHUNK), :],
                                     ys_hbm.at[chunk_rows(cid), :],
                                     sem_out.at[sl])

    def for_chunks(tile, fn):
        for i in range(CPT):
            cid = src_ref[tile * CPT + i]

            @pl.when(cid >= 0)
            def _():
                fn(cid, i)

    @pl.when(j == 0)
    def _():
        xbuf[...] = jnp.zeros(xbuf.shape, BF16)
        for_chunks(0, lambda cid, i: in_copy(cid, 0, i).start())

    @pl.when(j + 1 < nt)
    def _():
        for_chunks(j + 1, lambda cid, i: in_copy(cid, 1 - slot, i).start())

    @pl.when(j < nt)
    def _():
        for_chunks(j, lambda cid, i: in_copy(cid, slot, i).wait())
        prev = te_ref[jnp.maximum(j - 1, 0)]

        @pl.when((j == 0) | (te_ref[j] != prev))
        def _():
            w1b[...] = w1_ref[...].astype(BF16)
            w3b[...] = w3_ref[...].astype(BF16)
            w2b[...] = w2_ref[...].astype(BF16)

        xt = xbuf[slot]
        hid = (_silu(_dot(xt, w1b[...])) * _dot(xt, w3b[...])).astype(BF16)
        ybuf[slot] = _dot(hid, w2b[...]).astype(BF16)
        for_chunks(j, lambda cid, i: out_copy(cid, slot, i).start())

        @pl.when(j >= 1)
        def _():
            for_chunks(j - 1, lambda cid, i: out_copy(cid, 1 - slot, i).wait())

        @pl.when(j == nt - 1)
        def _():
            for_chunks(j, lambda cid, i: out_copy(cid, slot, i).wait())


def _experts(xs, te, src, nt, w1, w3, w2, n_tiles_max):
    d = xs.shape[1]
    grid_spec = pltpu.PrefetchScalarGridSpec(
        num_scalar_prefetch=3,
        grid=(n_tiles_max,),
        in_specs=[pl.BlockSpec(memory_space=pl.ANY),
                  pl.BlockSpec((None, d, D_EXPERT), lambda j, te, src, nt: (te[j], 0, 0)),
                  pl.BlockSpec((None, d, D_EXPERT), lambda j, te, src, nt: (te[j], 0, 0)),
                  pl.BlockSpec((None, D_EXPERT, d), lambda j, te, src, nt: (te[j], 0, 0))],
        out_specs=pl.BlockSpec(memory_space=pl.ANY),
        scratch_shapes=[pltpu.VMEM((2, TILE_M, d), BF16),
                        pltpu.VMEM((2, TILE_M, d), BF16),
                        pltpu.VMEM((d, D_EXPERT), BF16),
                        pltpu.VMEM((d, D_EXPERT), BF16),
                        pltpu.VMEM((D_EXPERT, d), BF16),
                        pltpu.SemaphoreType.DMA((2,)),
                        pltpu.SemaphoreType.DMA((2,))])
    return pl.pallas_call(
        _expert_kernel,
        grid_spec=grid_spec,
        out_shape=jax.ShapeDtypeStruct(xs.shape, xs.dtype),
        input_output_aliases={3: 0},
        compiler_params=pltpu.CompilerParams(
            dimension_semantics=("arbitrary",), vmem_limit_bytes=VMEM_LIMIT),
        name="experts",
    )(te, src, nt, xs, w1, w3, w2)


def _tile_plan(nch_f, n_tiles_max):
    nch = nch_f[:, :, 0].astype(jnp.int32)
    nb = nch.shape[0]
    choff = jnp.cumsum(nch, axis=1) - nch
    per_expert = jnp.sum(nch, axis=0)
    ntile = (per_expert + (CPT - 1)) // CPT
    tile_end = jnp.cumsum(ntile)
    tile_start = tile_end - ntile
    before = jnp.cumsum(nch, axis=0) - nch
    slot0 = tile_start[None, :] * CPT + before
    c = jnp.arange(TB // ROW_CHUNK, dtype=jnp.int32)[None, None, :]
    slots = slot0[:, :, None] + c
    valid = c < nch[:, :, None]
    ids = (jnp.arange(nb, dtype=jnp.int32)[:, None, None] * (R_BLK // ROW_CHUNK)
           + choff[:, :, None] + c)
    n_slots = n_tiles_max * CPT
    src = jnp.full((n_slots,), -1, jnp.int32).at[jnp.where(valid, slots, n_slots).reshape(-1)].set(
        ids.reshape(-1), mode="drop")
    te = jnp.searchsorted(tile_end, jnp.arange(n_tiles_max, dtype=jnp.int32), side="right")
    te = jnp.minimum(te, N_EXPERTS - 1).astype(jnp.int32)
    nt = tile_end[-1:].astype(jnp.int32)
    return te, src, nt


def _combine_kernel(x_ref, mod_ref, pg_ref, ys_ref, sw1_ref, sw3_ref, sw2_ref, lng_ref, lnb_ref,
                    o_ref, *, alpha):
    x = x_ref[...]
    sh2 = mod_ref[3:4, :]
    sc2 = mod_ref[4:5, :]
    g2 = mod_ref[5:6, :]
    h2b = (x * (1.0 + sc2) + sh2).astype(BF16)
    hid = (_silu(_dot(h2b, sw1_ref[...])) * _dot(h2b, sw3_ref[...])).astype(BF16)
    ffn = _dot(hid, sw2_ref[...])
    pg = pg_ref[...]
    lane = lax.broadcasted_iota(jnp.int32, (TB, 256), 1).astype(F32)
    pos_b = [jnp.broadcast_to(pg[:, kk:kk + 1], (TB, 256)) for kk in range(TOP_K)]
    gate_b = [jnp.broadcast_to(pg[:, TOP_K + kk:TOP_K + kk + 1], (TB, 256)) for kk in range(TOP_K)]
    for grp in range(N_GRP):
        ll = lane + float(grp * 256)
        wmat = jnp.zeros((TB, 256), F32)
        for kk in range(TOP_K):
            wmat = wmat + jnp.where(pos_b[kk] == ll, gate_b[kk], 0.0)
        ffn = ffn + _dot(wmat.astype(BF16), ys_ref[grp * 256:(grp + 1) * 256, :])
    o_ref[...] = _layer_norm(alpha * x + g2 * ffn, lng_ref[...], lnb_ref[...])


def _combine(x1, mod_l, pg, ys, sw1, sw3, sw2, lng, lnb, blocks_per_batch, alpha):
    t, d = x1.shape
    nb = t // TB
    return pl.pallas_call(
        functools.partial(_combine_kernel, alpha=alpha),
        grid=(nb,),
        in_specs=[pl.BlockSpec((TB, d), lambda i: (i, 0)),
                  pl.BlockSpec((None, 6, d), lambda i: (i // blocks_per_batch, 0, 0)),
                  pl.BlockSpec((None, TB, 128), lambda i: (i, 0, 0)),
                  pl.BlockSpec((R_BLK, d), lambda i: (i, 0)),
                  _full(sw1.shape), _full(sw3.shape), _full(sw2.shape),
                  _full(lng.shape), _full(lnb.shape)],
        out_specs=pl.BlockSpec((TB, d), lambda i: (i, 0)),
        out_shape=jax.ShapeDtypeStruct((t, d), F32),
        compiler_params=pltpu.CompilerParams(
            dimension_semantics=("arbitrary",), vmem_limit_bytes=VMEM_LIMIT),
        name="combine",
    )(x1, mod_l, pg, ys, sw1, sw3, sw2, lng, lnb)


def kernel(x, c, w_ada, b_ada, w_in, lru_conv_w, lru_conv_b, lru_w_a, lru_b_a, lru_w_x, lru_b_x, lru_lambda, gla_w_alpha, gla_b_alpha, gla_norm, ssd_conv_w, ssd_conv_b, ssd_dt_bias, ssd_a_log, ssd_d, ssd_norm, w_out, ln1_g, ln1_b, router_w, router_bias, exp_w1, exp_w3, exp_w2, shared_w1, shared_w3, shared_w2, ln2_g, ln2_b):
    bsz, seq, d = x.shape
    n_layers = w_ada.shape[0]
    assert d == D_MODEL and seq % TS == 0 and seq % TB == 0
    alpha = float((2 * n_layers) ** 0.25)
    t = bsz * seq
    nb = t // TB
    blocks_per_batch = seq // TB
    n_tiles_max = -(-(nb * (R_BLK // ROW_CHUNK)) // CPT) + N_EXPERTS

    mod = _adaln(c, w_ada, b_ada).reshape(n_layers, bsz, 6, d)

    o_r = 2 * LRU_WIDTH + 2 * GLA_QK + 2 * GLA_WIDTH
    o_z = o_r + GLA_RANK
    o_xbc = o_z + SSD_WIDTH
    o_dt = o_xbc + SSD_CONV_DIM
    win = jnp.concatenate(
        [w_in[..., :o_r], w_in[..., o_z:o_xbc], w_in[..., o_xbc:o_dt], w_in[..., o_r:o_z],
         w_in[..., o_dt:o_dt + SSD_HEADS],
         jnp.zeros((n_layers, d, 128 - GLA_RANK - SSD_HEADS), F32)], axis=-1).astype(BF16)
    eye = jnp.eye(LRU_BLOCKS, dtype=F32)
    bd = lambda w: jnp.einsum("lnij,nm->lnimj", w, eye).reshape(n_layers, LRU_WIDTH, LRU_WIDTH)
    wg = jnp.concatenate([bd(lru_w_a), bd(lru_w_x)], axis=-1).astype(BF16)
    bg = jnp.concatenate([lru_b_a, lru_b_x], axis=-1)[:, None, :]
    wal = jnp.concatenate([gla_w_alpha, jnp.zeros((n_layers, 128 - GLA_RANK, GLA_QK), F32)],
                          axis=1).astype(BF16)
    pad_dt = lambda v: jnp.pad(v, ((0, 0), (RD_DT0, 128 - RD_DT0 - SSD_HEADS)))[:, None, :]
    e_np = np.zeros((128, SSD_WIDTH), np.float32)
    for hh in range(SSD_HEADS):
        e_np[RD_DT0 + hh, hh * SSD_HEADDIM:(hh + 1) * SSD_HEADDIM] = 1.0
    e_mat = jnp.asarray(e_np, BF16)
    wout = w_out.astype(BF16)
    rwt = jnp.swapaxes(router_w, 1, 2)
    ustrict = jnp.asarray(np.triu(np.ones((TB, TB), np.float32), 1), BF16)
    lstrict = jnp.asarray(np.tril(np.ones((N_EXPERTS, N_EXPERTS), np.float32), -1), BF16)
    sw1 = shared_w1.astype(BF16)
    sw3 = shared_w3.astype(BF16)
    sw2 = shared_w2.astype(BF16)

    for l in range(n_layers):
        p = dict(win=win[l], lcw=lru_conv_w[l], lcb=lru_conv_b[l][None, :], wg=wg[l], bg=bg[l],
                 lam=lru_lambda[l][None, :], wal=wal[l], bal=gla_b_alpha[l][None, :],
                 gnorm=jnp.tile(gla_norm[l], GLA_HEADS)[None, :],
                 scw=ssd_conv_w[l], scb=ssd_conv_b[l][None, :],
                 dtb=pad_dt(ssd_dt_bias)[l], alog=pad_dt(ssd_a_log)[l],
                 dskip=jnp.repeat(ssd_d[l], SSD_HEADDIM)[None, :], snorm=ssd_norm[l][None, :],
                 e=e_mat, wout=wout[l], lng=ln1_g[l][None, :], lnb=ln1_b[l][None, :])
        x1 = _mixer(x, mod[l], p, alpha).reshape(t, d)
        xs, pg, nch = _router(x1, mod[l], rwt[l], router_bias[l][:, None], ustrict, lstrict,
                              blocks_per_batch)
        te, src, nt = _tile_plan(nch, n_tiles_max)
        ys = _experts(xs, te, src, nt, exp_w1[l], exp_w3[l], exp_w2[l], n_tiles_max)
        x = _combine(x1, mod[l], pg, ys, sw1[l], sw3[l], sw2[l], ln2_g[l][None, :],
                     ln2_b[l][None, :], blocks_per_batch, alpha).reshape(bsz, seq, d)
    return x
```

```python
import functools

import jax
import jax.numpy as jnp
import numpy as np
from jax import lax
from jax.experimental import pallas as pl
from jax.experimental.pallas import tpu as pltpu

F32 = jnp.float32
BF16 = jnp.bfloat16

D_MODEL = 1024
CHUNK = 64
LRU_WIDTH = 512
LRU_BLOCKS = 8
LRU_C = 8.0
CONV_W = 4
GLA_WIDTH = 512
GLA_HEADS = 4
GLA_DV = 128
GLA_DK = 64
GLA_QK = 256
GLA_RANK = 16
GLA_TAU = 16.0
SSD_WIDTH = 1024
SSD_HEADDIM = 64
SSD_HEADS = 16
SSD_GROUPS = 2
SSD_HPG = 8
SSD_STATE = 128
SSD_CONV_DIM = SSD_WIDTH + 2 * SSD_GROUPS * SSD_STATE
D_MIX = 2048
N_EXPERTS = 64
TOP_K = 8
N_EXPERT_GROUPS = 8
EXPERTS_PER_GROUP = 8
TOPK_GROUPS = 4
D_EXPERT = 256
ROUTE_SCALE = 2.5
LN_EPS = 1e-5
RMS_EPS = 1e-6

C_LRU_X = 0
C_LRU_G = 512
C_Q = 1024
C_K = 1280
C_V = 1536
C_G = 2048
C_Z = 2560
C_XBC = 3584
C_RD = 5120
D_IN_PAD = 5248
RD_DT0 = 16

TS = 256
TB = 256
ROW_CHUNK = 16
CPT = 32
TILE_M = ROW_CHUNK * CPT
R_BLK = -(-(TOP_K * TB + N_EXPERTS * (ROW_CHUNK - 1)) // 256) * 256
N_GRP = R_BLK // 256
VMEM_LIMIT = 56 * 1024 * 1024


def _dot(a, b):
    return jnp.dot(a, b, preferred_element_type=F32)


def _dot_nt(a, b):
    return lax.dot_general(a, b, (((1,), (1,)), ((), ())), preferred_element_type=F32)


def _dot_tn(a, b):
    return lax.dot_general(a, b, (((0,), (0,)), ((), ())), preferred_element_type=F32)


def _sigmoid(x):
    return jax.nn.sigmoid(x)


def _silu(x):
    return x * _sigmoid(x)


def _softplus(x):
    return jnp.maximum(x, 0.0) + jnp.log1p(jnp.exp(-jnp.abs(x)))


def _gelu_tanh(x):
    return 0.5 * x * (1.0 + jnp.tanh(0.7978845608028654 * (x + 0.044715 * (x * x * x))))


def _layer_norm(x, g, b):
    mu = jnp.mean(x, axis=-1, keepdims=True)
    xc = x - mu
    var = jnp.mean(xc * xc, axis=-1, keepdims=True)
    return xc * lax.rsqrt(var + LN_EPS) * g + b


def _shift_rows(x, d, fill, rowmod):
    return jnp.where(rowmod >= d, pltpu.roll(x, d, axis=0), fill)


def _cumsum_rows(x, period):
    rowmod = lax.broadcasted_iota(jnp.int32, x.shape, 0) & (period - 1)
    d = 1
    while d < period:
        x = x + _shift_rows(x, d, 0.0, rowmod)
        d *= 2
    return x


def _causal_conv(stage_ref, x, w_ref, b_ref, first):
    n = x.shape[0]

    @pl.when(first)
    def _():
        stage_ref[0:8, :] = jnp.zeros((8, x.shape[1]), F32)

    stage_ref[8:8 + n, :] = x
    y = b_ref[...] + w_ref[CONV_W - 1:CONV_W, :] * x
    for k in range(CONV_W - 1):
        y = y + w_ref[k:k + 1, :] * stage_ref[5 + k:5 + k + n, :]
    stage_ref[0:8, :] = stage_ref[n:n + 8, :]
    return y


def _adaln_kernel(c_ref, w_ref, b_ref, o_ref):
    cond = _silu(c_ref[...])
    o_ref[...] = jnp.dot(cond, w_ref[...], preferred_element_type=F32,
                         precision=lax.Precision.HIGHEST) + b_ref[...]


def _adaln(c, w_ada, b_ada):
    n_layers, d, n = w_ada.shape
    bsz = c.shape[0]
    tn = 1024
    return pl.pallas_call(
        _adaln_kernel,
        grid=(n_layers, n // tn),
        in_specs=[pl.BlockSpec((bsz, d), lambda l, j: (0, 0)),
                  pl.BlockSpec((None, d, tn), lambda l, j: (l, 0, j)),
                  pl.BlockSpec((None, 1, tn), lambda l, j: (l, 0, j))],
        out_specs=pl.BlockSpec((None, bsz, tn), lambda l, j: (l, 0, j)),
        out_shape=jax.ShapeDtypeStruct((n_layers, bsz, n), F32),
        name="adaln",
    )(c, w_ada, b_ada.reshape(n_layers, 1, n))


def _mixer_kernel(x_ref, mod_ref, win_ref, lcw_ref, lcb_ref, wg_ref, bg_ref, lam_ref,
                  wal_ref, bal_ref, gnorm_ref, scw_ref, scb_ref, dtb_ref, alog_ref,
                  dskip_ref, snorm_ref, e_ref, wout_ref, lng_ref, lnb_ref,
                  o_ref,
                  stage_l, stage_s, hcar, gla_st, ssd_st, o_scr, cat, *, alpha):
    first = pl.program_id(1) == 0
    x = x_ref[...]
    sh1 = mod_ref[0:1, :]
    sc1 = mod_ref[1:2, :]
    g1 = mod_ref[2:3, :]
    h = (x * (1.0 + sc1) + sh1).astype(BF16)

    @pl.when(first)
    def _():
        hcar[...] = jnp.zeros(hcar.shape, F32)
        gla_st[...] = jnp.zeros(gla_st.shape, F32)
        ssd_st[...] = jnp.zeros(ssd_st.shape, F32)

    row = lax.broadcasted_iota(jnp.int32, (TS, 1), 0)

    lx = _dot(h, win_ref[:, C_LRU_X:C_LRU_X + LRU_WIDTH])
    xc = _causal_conv(stage_l, lx, lcw_ref, lcb_ref, first)
    gts = _dot(xc.astype(BF16), wg_ref[...]) + bg_ref[...]
    r = _sigmoid(gts[:, :LRU_WIDTH])
    i_g = _sigmoid(gts[:, LRU_WIDTH:])
    log_a = (-LRU_C) * r * _softplus(-lam_ref[...])
    a = jnp.exp(log_a)
    u = jnp.sqrt(-jnp.tanh(log_a) * (a * a + 1.0)) * (i_g * xc)
    acc_a, acc_h = a, u
    d = 1
    while d < TS:
        sh_h = jnp.where(row >= d, pltpu.roll(acc_h, d, axis=0), 0.0)
        sh_a = jnp.where(row >= d, pltpu.roll(acc_a, d, axis=0), 1.0)
        acc_h = acc_h + acc_a * sh_h
        acc_a = acc_a * sh_a
        d *= 2
    h_lru = acc_h + acc_a * hcar[0:1, :]
    hcar[0:1, :] = h_lru[TS - 1:TS, :]
    lgate = _dot(h, win_ref[:, C_LRU_G:C_LRU_G + LRU_WIDTH])
    cat[:, 0:LRU_WIDTH] = (h_lru * _gelu_tanh(lgate)).astype(BF16)

    rd = _dot(h, win_ref[:, C_RD:C_RD + 128])

    q = _dot(h, win_ref[:, C_Q:C_Q + GLA_QK]) * (GLA_DK ** -0.5)
    k = _dot(h, win_ref[:, C_K:C_K + GLA_QK])
    v = _dot(h, win_ref[:, C_V:C_V + GLA_WIDTH]).astype(BF16)
    zal = _dot(rd.astype(BF16), wal_ref[...]) + bal_ref[...]
    log_alpha = (jnp.minimum(zal, 0.0) - jnp.log1p(jnp.exp(-jnp.abs(zal)))) * (1.0 / GLA_TAU)
    cum = _cumsum_rows(log_alpha, CHUNK)
    q_dec = q * jnp.exp(cum)
    k_dec = (k * jnp.exp(-cum)).astype(BF16)
    ri = lax.broadcasted_iota(jnp.int32, (TS, TS), 0)
    ci = lax.broadcasted_iota(jnp.int32, (TS, TS), 1)
    causal = ri >= ci
    blk_causal = causal & ((ri // CHUNK) == (ci // CHUNK))
    lane_head = lax.broadcasted_iota(jnp.int32, (1, GLA_QK), 1) // GLA_DK
    for hd in range(GLA_HEADS):
        qm = jnp.where(lane_head == hd, q_dec, 0.0).astype(BF16)
        att = jnp.where(blk_causal, _dot_nt(qm, k_dec), 0.0).astype(BF16)
        o_scr[:, hd * GLA_DV:(hd + 1) * GLA_DV] = _dot(att, v[:, hd * GLA_DV:(hd + 1) * GLA_DV])
    sr = lax.broadcasted_iota(jnp.int32, (GLA_WIDTH, GLA_QK), 0) // GLA_DV
    scol = lax.broadcasted_iota(jnp.int32, (GLA_WIDTH, GLA_QK), 1) // GLA_DK
    st_mask = sr == scol
    q_dec_b = q_dec.astype(BF16)
    for c in range(TS // CHUNK):
        rows = slice(c * CHUNK, (c + 1) * CHUNK)
        st = gla_st[...]
        o_scr[rows, :] = o_scr[rows, :] + _dot_nt(q_dec_b[rows, :], st.astype(BF16))
        cum_c = cum[rows, :]
        cum_last = cum_c[CHUNK - 1:CHUNK, :]
        k_end = (k[rows, :] * jnp.exp(cum_last - cum_c)).astype(BF16)
        upd = _dot_tn(v[rows, :], k_end)
        gla_st[...] = st * jnp.exp(cum_last) + jnp.where(st_mask, upd, 0.0)
    g_gate = _dot(h, win_ref[:, C_G:C_G + GLA_WIDTH])
    for hd in range(GLA_HEADS):
        cols = slice(hd * GLA_DV, (hd + 1) * GLA_DV)
        oh = o_scr[:, cols]
        ms = jnp.mean(oh * oh, axis=-1, keepdims=True)
        on = oh * lax.rsqrt(ms + RMS_EPS) * gnorm_ref[:, cols]
        cat[:, LRU_WIDTH + hd * GLA_DV:LRU_WIDTH + (hd + 1) * GLA_DV] = (
            on * _silu(g_gate[:, cols])).astype(BF16)

    xbc = _dot(h, win_ref[:, C_XBC:C_XBC + SSD_CONV_DIM])
    xbc = _silu(_causal_conv(stage_s, xbc, scw_ref, scb_ref, first))
    xs = xbc[:, :SSD_WIDTH]
    lane128 = lax.broadcasted_iota(jnp.int32, (1, 128), 1)
    dt_lane = (lane128 >= RD_DT0) & (lane128 < RD_DT0 + SSD_HEADS)
    a_neg = jnp.where(dt_lane, -jnp.exp(alog_ref[...]), 0.0)
    dtp = _softplus(rd + dtb_ref[...])
    a_cs = _cumsum_rows(dtp * a_neg, TS)
    a_last = a_cs[TS - 1:TS, :]
    ea = jnp.exp(a_cs)
    dte = jnp.exp(a_last - a_cs)
    expand = _dot(jnp.concatenate([dtp, dte, ea], axis=0).astype(BF16), e_ref[...])
    dt_x = expand[0:TS, :]
    dte_x = expand[TS:2 * TS, :]
    ea_x = expand[2 * TS:3 * TS, :]
    xdt = xs * dt_x
    xdt_b = xdt.astype(BF16)
    xdte_b = (xdt * dte_x).astype(BF16)
    a_cs_t = a_cs.T
    z = _dot(h, win_ref[:, C_Z:C_Z + SSD_WIDTH])
    lane_lo = lax.broadcasted_iota(jnp.int32, (1, 128), 1) < SSD_HEADDIM
    gw = SSD_HPG * SSD_HEADDIM
    for g in range(SSD_GROUPS):
        bm = xbc[:, SSD_WIDTH + g * SSD_STATE:SSD_WIDTH + (g + 1) * SSD_STATE].astype(BF16)
        cm = xbc[:, SSD_WIDTH + (SSD_GROUPS + g) * SSD_STATE:
                 SSD_WIDTH + (SSD_GROUPS + g + 1) * SSD_STATE].astype(BF16)
        cb = _dot_nt(cm, bm)
        st = ssd_st[g]
        y_off = _dot(cm, st.astype(BF16)) * ea_x[:, g * gw:(g + 1) * gw]
        yg = []
        for j in range(SSD_HPG // 2):
            ys = []
            for hh in (2 * j, 2 * j + 1):
                lane = RD_DT0 + g * SSD_HPG + hh
                diff = a_cs[:, lane:lane + 1] - a_cs_t[lane:lane + 1, :]
                m = (cb * jnp.exp(jnp.where(causal, diff, -jnp.inf))).astype(BF16)
                ys.append(_dot(m, xdt_b[:, g * gw + j * 128:g * gw + (j + 1) * 128]))
            cols = slice(g * gw + j * 128, g * gw + (j + 1) * 128)
            y = jnp.where(lane_lo, ys[0], ys[1]) + y_off[:, j * 128:(j + 1) * 128]
            y = y + xs[:, cols] * dskip_ref[:, cols]
            yg.append(y * _silu(z[:, cols]))
        ssd_st[g] = st * ea_x[TS - 1:TS, g * gw:(g + 1) * gw] + _dot_tn(bm, xdte_b[:, g * gw:(g + 1) * gw])
        ssq = yg[0] * yg[0]
        for t in yg[1:]:
            ssq = ssq + t * t
        ms = jnp.sum(ssq, axis=-1, keepdims=True) * (1.0 / gw)
        inv = lax.rsqrt(ms + RMS_EPS)
        for j, t in enumerate(yg):
            cols = slice(g * gw + j * 128, g * gw + (j + 1) * 128)
            cat[:, LRU_WIDTH + GLA_WIDTH + g * gw + j * 128:
                LRU_WIDTH + GLA_WIDTH + g * gw + (j + 1) * 128] = (
                    t * inv * snorm_ref[:, cols]).astype(BF16)

    mix = _dot(cat[...], wout_ref[...])
    o_ref[...] = _layer_norm(alpha * x + g1 * mix, lng_ref[...], lnb_ref[...])


def _full(shape):
    nd = len(shape)
    return pl.BlockSpec(shape, lambda *_: (0,) * nd)


def _mixer(x, mod_l, p, alpha):
    bsz, seq, d = x.shape
    small = [p["lcw"], p["lcb"], p["wg"], p["bg"], p["lam"], p["wal"], p["bal"], p["gnorm"],
             p["scw"], p["scb"], p["dtb"], p["alog"], p["dskip"], p["snorm"], p["e"],
             p["wout"], p["lng"], p["lnb"]]
    in_specs = [pl.BlockSpec((None, TS, d), lambda b, s: (b, s, 0)),
                pl.BlockSpec((None, 6, d), lambda b, s: (b, 0, 0)),
                _full(p["win"].shape)] + [_full(a.shape) for a in small]
    return pl.pallas_call(
        functools.partial(_mixer_kernel, alpha=alpha),
        grid=(bsz, seq // TS),
        in_specs=in_specs,
        out_specs=pl.BlockSpec((None, TS, d), lambda b, s: (b, s, 0)),
        out_shape=jax.ShapeDtypeStruct((bsz, seq, d), F32),
        scratch_shapes=[pltpu.VMEM((TS + 8, LRU_WIDTH), F32),
                        pltpu.VMEM((TS + 8, SSD_CONV_DIM), F32),
                        pltpu.VMEM((8, LRU_WIDTH), F32),
                        pltpu.VMEM((GLA_WIDTH, GLA_QK), F32),
                        pltpu.VMEM((SSD_GROUPS, SSD_STATE, SSD_HPG * SSD_HEADDIM), F32),
                        pltpu.VMEM((TS, GLA_WIDTH), F32),
                        pltpu.VMEM((TS, D_MIX), BF16)],
        compiler_params=pltpu.CompilerParams(
            dimension_semantics=("arbitrary", "arbitrary"), vmem_limit_bytes=VMEM_LIMIT),
        name="mixer",
    )(x, mod_l, p["win"], *small)


def _router_kernel(*refs):
    last = pl.program_id(0) == pl.num_programs(0) - 1

    @pl.when(jnp.logical_not(last))
    def _():
        _route_block(*refs)

    @pl.when(last)
    def _():
        for o_ref in refs[6:9]:
            o_ref[...] = jnp.zeros(o_ref.shape, o_ref.dtype)


def _route_block(x_ref, mod_ref, rwt_ref, rbias_ref, ustrict_ref, lstrict_ref,
                 xs_ref, pg_ref, nch_ref, msk_scr, pg_scr):
    x = x_ref[...]
    sh2 = mod_ref[3:4, :]
    sc2 = mod_ref[4:5, :]
    h2 = x * (1.0 + sc2) + sh2
    logits = lax.dot_general(rwt_ref[...], h2, (((1,), (1,)), ((), ())),
                             preferred_element_type=F32, precision=lax.Precision.HIGHEST)
    scores = _sigmoid(logits)
    biased = scores + rbias_ref[...]

    row8 = lax.broadcasted_iota(jnp.int32, (EXPERTS_PER_GROUP, TB), 0)
    gsc = []
    for g in range(N_EXPERT_GROUPS):
        vals = biased[g * EXPERTS_PER_GROUP:(g + 1) * EXPERTS_PER_GROUP, :]
        m1 = jnp.max(vals, axis=0, keepdims=True)
        i1 = jnp.min(jnp.where(vals == m1, row8, EXPERTS_PER_GROUP), axis=0, keepdims=True)
        m2 = jnp.max(jnp.where(row8 == i1, -jnp.inf, vals), axis=0, keepdims=True)
        gsc.append(m1 + m2)
    for g in range(N_EXPERT_GROUPS):
        beaten = jnp.zeros((1, TB), F32)
        for o in range(N_EXPERT_GROUPS):
            if o == g:
                continue
            wins = (gsc[o] >= gsc[g]) if o < g else (gsc[o] > gsc[g])
            beaten = beaten + jnp.where(wins, 1.0, 0.0)
        keep = beaten < TOPK_GROUPS
        vals = biased[g * EXPERTS_PER_GROUP:(g + 1) * EXPERTS_PER_GROUP, :]
        msk_scr[g * EXPERTS_PER_GROUP:(g + 1) * EXPERTS_PER_GROUP, :] = jnp.where(keep, vals, -jnp.inf)

    cur = msk_scr[...]
    row64 = lax.broadcasted_iota(jnp.int32, (N_EXPERTS, TB), 0)
    sel = jnp.zeros((N_EXPERTS, TB), F32)
    picks = []
    for _ in range(TOP_K):
        m = jnp.max(cur, axis=0, keepdims=True)
        ik = jnp.min(jnp.where(cur == m, row64, N_EXPERTS), axis=0, keepdims=True)
        hit = row64 == ik
        sel = jnp.where(hit, 1.0, sel)
        cur = jnp.where(hit, -jnp.inf, cur)
        picks.append(hit)
    wsel = sel * scores
    gates = wsel * (ROUTE_SCALE / jnp.sum(wsel, axis=0, keepdims=True))

    rank = _dot(sel.astype(BF16), ustrict_ref[...])
    cnt = jnp.sum(sel, axis=1, keepdims=True)
    nch = jnp.floor((cnt + (ROW_CHUNK - 1.0)) * (1.0 / ROW_CHUNK))
    nch_b = jnp.broadcast_to(nch, (N_EXPERTS, TB))
    choff = _dot(lstrict_ref[...], nch_b.astype(BF16))
    pos = choff * float(ROW_CHUNK) + rank
    nch_ref[...] = jnp.broadcast_to(nch, (N_EXPERTS, 128))

    pg_scr[...] = jnp.zeros(pg_scr.shape, F32)
    pos_k = []
    for kk, hit in enumerate(picks):
        pk = jnp.sum(jnp.where(hit, pos, 0.0), axis=0, keepdims=True)
        gk = jnp.sum(jnp.where(hit, gates, 0.0), axis=0, keepdims=True)
        pos_k.append(pk)
        pg_scr[kk:kk + 1, :] = pk
        pg_scr[TOP_K + kk:TOP_K + kk + 1, :] = gk
    pg_ref[...] = pg_scr[...].T

    h2b = h2.astype(BF16)
    riota = lax.broadcasted_iota(jnp.int32, (256, TB), 0).astype(F32)
    for grp in range(N_GRP):
        rr = riota + float(grp * 256)
        onehot = jnp.zeros((256, TB), F32)
        for pk in pos_k:
            onehot = jnp.where(pk == rr, 1.0, onehot)
        xs_ref[grp * 256:(grp + 1) * 256, :] = _dot(onehot.astype(BF16), h2b).astype(BF16)


def _router(x1, mod_l, rwt, rbias, ustrict, lstrict, blocks_per_batch):
    t, d = x1.shape
    nb = t // TB
    return pl.pallas_call(
        _router_kernel,
        grid=(nb + 1,),
        in_specs=[pl.BlockSpec((TB, d), lambda i: (jnp.minimum(i, nb - 1), 0)),
                  pl.BlockSpec((None, 6, d),
                               lambda i: (jnp.minimum(i, nb - 1) // blocks_per_batch, 0, 0)),
                  _full(rwt.shape), _full(rbias.shape), _full(ustrict.shape), _full(lstrict.shape)],
        out_specs=[pl.BlockSpec((R_BLK, d), lambda i: (i, 0)),
                   pl.BlockSpec((None, TB, 128), lambda i: (i, 0, 0)),
                   pl.BlockSpec((None, N_EXPERTS, 128), lambda i: (i, 0, 0))],
        out_shape=[jax.ShapeDtypeStruct(((nb + 1) * R_BLK, d), BF16),
                   jax.ShapeDtypeStruct((nb + 1, TB, 128), F32),
                   jax.ShapeDtypeStruct((nb + 1, N_EXPERTS, 128), F32)],
        scratch_shapes=[pltpu.VMEM((N_EXPERTS, TB), F32), pltpu.VMEM((128, TB), F32)],
        compiler_params=pltpu.CompilerParams(
            dimension_semantics=("arbitrary",), vmem_limit_bytes=VMEM_LIMIT),
        name="router",
    )(x1, mod_l, rwt, rbias, ustrict, lstrict)


def _expert_kernel(te_ref, src_ref, dst_ref, nt_ref, xs_hbm, w1_ref, w3_ref, w2_ref, ys_hbm,
                   xbuf, ybuf, w1b, w3b, w2b, sem_in, sem_out):
    j = pl.program_id(0)
    nt = nt_ref[0]
    slot = lax.rem(j, 2)

    def chunk_rows(cid):
        return pl.ds(pl.multiple_of(cid * ROW_CHUNK, ROW_CHUNK), ROW_CHUNK)

    def in_copy(cid, sl, i):
        return pltpu.make_async_copy(xs_hbm.at[chunk_rows(cid), :],
                                     xbuf.at[sl, pl.ds(i * ROW_CHUNK, ROW_CHUNK), :],
                                     sem_in.at[sl])

    def out_copy(cid, sl, i):
        return pltpu.make_async_copy(ybuf.at[sl, pl.ds(i * ROW_CHUNK, ROW_CHUNK), :],
                                     ys_hbm.at[chunk_rows(cid), :],
                                     sem_out.at[sl])

    def start_in(tile, sl):
        for i in range(CPT):
            in_copy(src_ref[tile * CPT + i], sl, i).start()

    def wait_in(sl):
        for i in range(CPT):
            in_copy(0, sl, i).wait()

    def start_out(tile, sl):
        for i in range(CPT):
            out_copy(dst_ref[tile * CPT + i], sl, i).start()

    def wait_out(sl):
        for i in range(CPT):
            out_copy(0, sl, i).wait()

    @pl.when(j == 0)
    def _():
        start_in(0, 0)

    @pl.when(j + 1 < nt)
    def _():
        start_in(j + 1, 1 - slot)

    @pl.when(j < nt)
    def _():
        wait_in(slot)
        prev = te_ref[jnp.maximum(j - 1, 0)]

        @pl.when((j == 0) | (te_ref[j] != prev))
        def _():
            w1b[...] = w1_ref[...].astype(BF16)
            w3b[...] = w3_ref[...].astype(BF16)
            w2b[...] = w2_ref[...].astype(BF16)

        xt = xbuf[slot]
        hid = (_silu(_dot(xt, w1b[...])) * _dot(xt, w3b[...])).astype(BF16)
        ybuf[slot] = _dot(hid, w2b[...]).astype(BF16)
        start_out(j, slot)

        @pl.when(j >= 1)
        def _():
            wait_out(1 - slot)

        @pl.when(j == nt - 1)
        def _():
            wait_out(slot)


def _experts(xs, te, src, dst, nt, w1, w3, w2, n_tiles_max):
    d = xs.shape[1]
    grid_spec = pltpu.PrefetchScalarGridSpec(
        num_scalar_prefetch=4,
        grid=(n_tiles_max,),
        in_specs=[pl.BlockSpec(memory_space=pl.ANY),
                  pl.BlockSpec((None, d, D_EXPERT), lambda j, te, *_: (te[j], 0, 0)),
                  pl.BlockSpec((None, d, D_EXPERT), lambda j, te, *_: (te[j], 0, 0)),
                  pl.BlockSpec((None, D_EXPERT, d), lambda j, te, *_: (te[j], 0, 0))],
        out_specs=pl.BlockSpec(memory_space=pl.ANY),
        scratch_shapes=[pltpu.VMEM((2, TILE_M, d), BF16),
                        pltpu.VMEM((2, TILE_M, d), BF16),
                        pltpu.VMEM((d, D_EXPERT), BF16),
                        pltpu.VMEM((d, D_EXPERT), BF16),
                        pltpu.VMEM((D_EXPERT, d), BF16),
                        pltpu.SemaphoreType.DMA((2,)),
                        pltpu.SemaphoreType.DMA((2,))])
    return pl.pallas_call(
        _expert_kernel,
        grid_spec=grid_spec,
        out_shape=jax.ShapeDtypeStruct(xs.shape, xs.dtype),
        input_output_aliases={4: 0},
        compiler_params=pltpu.CompilerParams(
            dimension_semantics=("arbitrary",), vmem_limit_bytes=VMEM_LIMIT),
        name="experts",
    )(te, src, dst, nt, xs, w1, w3, w2)


def _tile_plan(nch_f, n_tiles_max):
    nch = nch_f[:-1, :, 0]
    nb = nch.shape[0]
    hi = lax.Precision.HIGHEST
    choff = jnp.cumsum(nch, axis=1) - nch
    per_expert = jnp.sum(nch, axis=0)
    ntile = jnp.floor((per_expert + (CPT - 1.0)) * (1.0 / CPT))
    tile_end = jnp.cumsum(ntile)
    tile_start = tile_end - ntile
    nt = tile_end[-1]
    jt = jnp.arange(n_tiles_max, dtype=F32)
    te = jnp.minimum(jnp.sum((tile_end[None, :] <= jt[:, None]).astype(F32), axis=1), N_EXPERTS - 1.0)
    oh = (te[:, None] == jnp.arange(N_EXPERTS, dtype=F32)[None, :]).astype(F32)
    cum_t = jnp.dot(oh, jnp.cumsum(nch, axis=0).T, precision=hi)
    choff_t = jnp.dot(oh, choff.T, precision=hi)
    start_t = jnp.dot(oh, tile_start, precision=hi)
    total_t = jnp.dot(oh, per_expert, precision=hi)
    q = (jt - start_t)[:, None] * CPT + jnp.arange(CPT, dtype=F32)[None, :]
    le = cum_t[:, None, :] <= q[:, :, None]
    b = jnp.minimum(jnp.sum(le.astype(F32), axis=2), nb - 1.0)
    before = jnp.max(jnp.where(le, cum_t[:, None, :], 0.0), axis=2)
    is_b = jnp.arange(nb, dtype=F32)[None, None, :] == b[:, :, None]
    choff_s = jnp.sum(jnp.where(is_b, choff_t[:, None, :], 0.0), axis=2)
    valid = (jt[:, None] < nt) & (q < total_t[:, None])
    cid = b * (R_BLK // ROW_CHUNK) + choff_s + (q - before)
    spare = nb * (R_BLK // ROW_CHUNK) + jnp.mod(jt, 2.0)[:, None] * CPT + jnp.arange(CPT, dtype=F32)[None, :]
    src = jnp.where(valid, cid, 0.0).astype(jnp.int32).reshape(-1)
    dst = jnp.where(valid, cid, spare).astype(jnp.int32).reshape(-1)
    return te.astype(jnp.int32), src, dst, nt.astype(jnp.int32).reshape(1)


def _combine_kernel(x_ref, mod_ref, pg_ref, ys_ref, sw1_ref, sw3_ref, sw2_ref, lng_ref, lnb_ref,
                    o_ref, *, alpha):
    x = x_ref[...]
    sh2 = mod_ref[3:4, :]
    sc2 = mod_ref[4:5, :]
    g2 = mod_ref[5:6, :]
    h2b = (x * (1.0 + sc2) + sh2).astype(BF16)
    hid = (_silu(_dot(h2b, sw1_ref[...])) * _dot(h2b, sw3_ref[...])).astype(BF16)
    ffn = _dot(hid, sw2_ref[...])
    pg = pg_ref[...]
    lane = lax.broadcasted_iota(jnp.int32, (TB, 256), 1).astype(F32)
    pos_b = [jnp.broadcast_to(pg[:, kk:kk + 1], (TB, 256)) for kk in range(TOP_K)]
    gate_b = [jnp.broadcast_to(pg[:, TOP_K + kk:TOP_K + kk + 1], (TB, 256)) for kk in range(TOP_K)]
    for grp in range(N_GRP):
        ll = lane + float(grp * 256)
        wmat = jnp.zeros((TB, 256), F32)
        for kk in range(TOP_K):
            wmat = wmat + jnp.where(pos_b[kk] == ll, gate_b[kk], 0.0)
        ffn = ffn + _dot(wmat.astype(BF16), ys_ref[grp * 256:(grp + 1) * 256, :])
    o_ref[...] = _layer_norm(alpha * x + g2 * ffn, lng_ref[...], lnb_ref[...])


def _combine(x1, mod_l, pg, ys, sw1, sw3, sw2, lng, lnb, blocks_per_batch, alpha):
    t, d = x1.shape
    nb = t // TB
    return pl.pallas_call(
        functools.partial(_combine_kernel, alpha=alpha),
        grid=(nb,),
        in_specs=[pl.BlockSpec((TB, d), lambda i: (i, 0)),
                  pl.BlockSpec((None, 6, d), lambda i: (i // blocks_per_batch, 0, 0)),
                  pl.BlockSpec((None, TB, 128), lambda i: (i, 0, 0)),
                  pl.BlockSpec((R_BLK, d), lambda i: (i, 0)),
                  _full(sw1.shape), _full(sw3.shape), _full(sw2.shape),
                  _full(lng.shape), _full(lnb.shape)],
        out_specs=pl.BlockSpec((TB, d), lambda i: (i, 0)),
        out_shape=jax.ShapeDtypeStruct((t, d), F32),
        compiler_params=pltpu.CompilerParams(
            dimension_semantics=("arbitrary",), vmem_limit_bytes=VMEM_LIMIT),
        name="combine",
    )(x1, mod_l, pg, ys, sw1, sw3, sw2, lng, lnb)


def kernel(x, c, w_ada, b_ada, w_in, lru_conv_w, lru_conv_b, lru_w_a, lru_b_a, lru_w_x, lru_b_x, lru_lambda, gla_w_alpha, gla_b_alpha, gla_norm, ssd_conv_w, ssd_conv_b, ssd_dt_bias, ssd_a_log, ssd_d, ssd_norm, w_out, ln1_g, ln1_b, router_w, router_bias, exp_w1, exp_w3, exp_w2, shared_w1, shared_w3, shared_w2, ln2_g, ln2_b):
    bsz, seq, d = x.shape
    n_layers = w_ada.shape[0]
    assert d == D_MODEL and seq % TS == 0 and seq % TB == 0
    alpha = float((2 * n_layers) ** 0.25)
    t = bsz * seq
    nb = t // TB
    blocks_per_batch = seq // TB
    n_tiles_max = -(-(nb * (R_BLK // ROW_CHUNK)) // CPT) + N_EXPERTS

    mod = _adaln(c, w_ada, b_ada).reshape(n_layers, bsz, 6, d)

    o_r = 2 * LRU_WIDTH + 2 * GLA_QK + 2 * GLA_WIDTH
    o_z = o_r + GLA_RANK
    o_xbc = o_z + SSD_WIDTH
    o_dt = o_xbc + SSD_CONV_DIM
    win = jnp.concatenate(
        [w_in[..., :o_r], w_in[..., o_z:o_xbc], w_in[..., o_xbc:o_dt], w_in[..., o_r:o_z],
         w_in[..., o_dt:o_dt + SSD_HEADS],
         jnp.zeros((n_layers, d, 128 - GLA_RANK - SSD_HEADS), F32)], axis=-1).astype(BF16)
    eye = jnp.eye(LRU_BLOCKS, dtype=F32)
    bd = lambda w: jnp.einsum("lnij,nm->lnimj", w, eye).reshape(n_layers, LRU_WIDTH, LRU_WIDTH)
    wg = jnp.concatenate([bd(lru_w_a), bd(lru_w_x)], axis=-1).astype(BF16)
    bg = jnp.concatenate([lru_b_a, lru_b_x], axis=-1)[:, None, :]
    wal = jnp.concatenate([gla_w_alpha, jnp.zeros((n_layers, 128 - GLA_RANK, GLA_QK), F32)],
                          axis=1).astype(BF16)
    pad_dt = lambda v: jnp.pad(v, ((0, 0), (RD_DT0, 128 - RD_DT0 - SSD_HEADS)))[:, None, :]
    e_np = np.zeros((128, SSD_WIDTH), np.float32)
    for hh in range(SSD_HEADS):
        e_np[RD_DT0 + hh, hh * SSD_HEADDIM:(hh + 1) * SSD_HEADDIM] = 1.0
    e_mat = jnp.asarray(e_np, BF16)
    wout = w_out.astype(BF16)
    rwt = jnp.swapaxes(router_w, 1, 2)
    ustrict = jnp.asarray(np.triu(np.ones((TB, TB), np.float32), 1), BF16)
    lstrict = jnp.asarray(np.tril(np.ones((N_EXPERTS, N_EXPERTS), np.float32), -1), BF16)
    sw1 = shared_w1.astype(BF16)
    sw3 = shared_w3.astype(BF16)
    sw2 = shared_w2.astype(BF16)

    for l in range(n_layers):
        p = dict(win=win[l], lcw=lru_conv_w[l], lcb=lru_conv_b[l][None, :], wg=wg[l], bg=bg[l],
                 lam=lru_lambda[l][None, :], wal=wal[l], bal=gla_b_alpha[l][None, :],
                 gnorm=jnp.tile(gla_norm[l], GLA_HEADS)[None, :],
                 scw=ssd_conv_w[l], scb=ssd_conv_b[l][None, :],
                 dtb=pad_dt(ssd_dt_bias)[l], alog=pad_dt(ssd_a_log)[l],
                 dskip=jnp.repeat(ssd_d[l], SSD_HEADDIM)[None, :], snorm=ssd_norm[l][None, :],
                 e=e_mat, wout=wout[l], lng=ln1_g[l][None, :], lnb=ln1_b[l][None, :])
        x1 = _mixer(x, mod[l], p, alpha).reshape(t, d)
        xs, pg, nch = _router(x1, mod[l], rwt[l], router_bias[l][:, None], ustrict, lstrict,
                              blocks_per_batch)
        te, src, dst, nt = _tile_plan(nch, n_tiles_max)
        ys = _experts(xs, te, src, dst, nt, exp_w1[l], exp_w3[l], exp_w2[l], n_tiles_max)
        x = _combine(x1, mod[l], pg, ys, sw1[l], sw3[l], sw2[l], ln2_g[l][None, :],
                     ln2_b[l][None, :], blocks_per_batch, alpha).reshape(bsz, seq, d)
    return x
```

```python
import functools

import jax
import jax.numpy as jnp
import numpy as np
from jax import lax
from jax.experimental import pallas as pl
from jax.experimental.pallas import tpu as pltpu

F32 = jnp.float32
BF16 = jnp.bfloat16

D_MODEL = 1024
CHUNK = 64
LRU_WIDTH = 512
LRU_BLOCKS = 8
LRU_C = 8.0
CONV_W = 4
GLA_WIDTH = 512
GLA_HEADS = 4
GLA_DV = 128
GLA_DK = 64
GLA_QK = 256
GLA_RANK = 16
GLA_TAU = 16.0
SSD_WIDTH = 1024
SSD_HEADDIM = 64
SSD_HEADS = 16
SSD_GROUPS = 2
SSD_HPG = 8
SSD_STATE = 128
SSD_CONV_DIM = SSD_WIDTH + 2 * SSD_GROUPS * SSD_STATE
D_MIX = 2048
N_EXPERTS = 64
TOP_K = 8
N_EXPERT_GROUPS = 8
EXPERTS_PER_GROUP = 8
TOPK_GROUPS = 4
D_EXPERT = 256
ROUTE_SCALE = 2.5
LN_EPS = 1e-5
RMS_EPS = 1e-6

C_LRU_X = 0
C_LRU_G = 512
C_Q = 1024
C_K = 1280
C_V = 1536
C_G = 2048
C_Z = 2560
C_XBC = 3584
C_RD = 5120
D_IN_PAD = 5248
RD_DT0 = 16

TS = 256
TB = 256
ROW_CHUNK = 16
CPT = 32
TILE_M = ROW_CHUNK * CPT
NBUF = 4
R_BLK = -(-(TOP_K * TB + N_EXPERTS * (ROW_CHUNK - 1)) // 256) * 256
N_GRP = R_BLK // 256
VMEM_LIMIT = 56 * 1024 * 1024


def _dot(a, b):
    return jnp.dot(a, b, preferred_element_type=F32)


def _dot_nt(a, b):
    return lax.dot_general(a, b, (((1,), (1,)), ((), ())), preferred_element_type=F32)


def _dot_tn(a, b):
    return lax.dot_general(a, b, (((0,), (0,)), ((), ())), preferred_element_type=F32)


def _sigmoid(x):
    return jax.nn.sigmoid(x)


def _silu(x):
    return x * _sigmoid(x)


def _softplus(x):
    return jnp.maximum(x, 0.0) + jnp.log1p(jnp.exp(-jnp.abs(x)))


def _gelu_tanh(x):
    return 0.5 * x * (1.0 + jnp.tanh(0.7978845608028654 * (x + 0.044715 * (x * x * x))))


def _layer_norm(x, g, b):
    mu = jnp.mean(x, axis=-1, keepdims=True)
    xc = x - mu
    var = jnp.mean(xc * xc, axis=-1, keepdims=True)
    return xc * lax.rsqrt(var + LN_EPS) * g + b


def _shift_rows(x, d, fill, rowmod):
    return jnp.where(rowmod >= d, pltpu.roll(x, d, axis=0), fill)


def _cumsum_rows(x, period):
    rowmod = lax.broadcasted_iota(jnp.int32, x.shape, 0) & (period - 1)
    d = 1
    while d < period:
        x = x + _shift_rows(x, d, 0.0, rowmod)
        d *= 2
    return x


def _causal_conv(stage_ref, x, w_ref, b_ref, first):
    n = x.shape[0]

    @pl.when(first)
    def _():
        stage_ref[0:8, :] = jnp.zeros((8, x.shape[1]), F32)

    stage_ref[8:8 + n, :] = x
    y = b_ref[...] + w_ref[CONV_W - 1:CONV_W, :] * x
    for k in range(CONV_W - 1):
        y = y + w_ref[k:k + 1, :] * stage_ref[5 + k:5 + k + n, :]
    stage_ref[0:8, :] = stage_ref[n:n + 8, :]
    return y


def _adaln_kernel(c_ref, w_ref, b_ref, o_ref):
    cond = _silu(c_ref[...])
    o_ref[...] = jnp.dot(cond, w_ref[...], preferred_element_type=F32,
                         precision=lax.Precision.HIGHEST) + b_ref[...]


def _adaln(c, w_ada, b_ada):
    n_layers, d, n = w_ada.shape
    bsz = c.shape[0]
    tn = 1024
    return pl.pallas_call(
        _adaln_kernel,
        grid=(n_layers, n // tn),
        in_specs=[pl.BlockSpec((bsz, d), lambda l, j: (0, 0)),
                  pl.BlockSpec((None, d, tn), lambda l, j: (l, 0, j)),
                  pl.BlockSpec((None, 1, tn), lambda l, j: (l, 0, j))],
        out_specs=pl.BlockSpec((None, bsz, tn), lambda l, j: (l, 0, j)),
        out_shape=jax.ShapeDtypeStruct((n_layers, bsz, n), F32),
        name="adaln",
    )(c, w_ada, b_ada.reshape(n_layers, 1, n))


O_R = 2 * LRU_WIDTH + 2 * GLA_QK + 2 * GLA_WIDTH
O_Z = O_R + GLA_RANK
O_DT = O_Z + SSD_WIDTH + SSD_CONV_DIM
D_IN_PROJ = O_DT + SSD_HEADS


def _relayout_kernel(w_ref, o_ref):
    o_ref[:, 0:O_R] = w_ref[:, 0:O_R].astype(BF16)
    o_ref[:, C_Z:C_RD] = w_ref[:, O_Z:O_DT].astype(BF16)
    tail = jnp.concatenate(
        [w_ref[:, O_R:O_Z], w_ref[:, O_DT:D_IN_PROJ],
         jnp.zeros((w_ref.shape[0], 128 - GLA_RANK - SSD_HEADS), F32)], axis=1)
    o_ref[:, C_RD:D_IN_PAD] = tail.astype(BF16)


def _relayout_w_in(w_in):
    n_layers, d, n = w_in.shape
    assert n == D_IN_PROJ and O_R == C_Z and C_RD - C_Z == O_DT - O_Z
    rows = 256
    return pl.pallas_call(
        _relayout_kernel,
        grid=(n_layers, d // rows),
        in_specs=[pl.BlockSpec((None, rows, n), lambda l, i: (l, i, 0))],
        out_specs=pl.BlockSpec((None, rows, D_IN_PAD), lambda l, i: (l, i, 0)),
        out_shape=jax.ShapeDtypeStruct((n_layers, d, D_IN_PAD), BF16),
        name="relayout_w_in",
    )(w_in)


def _mixer_kernel(x_ref, mod_ref, win_ref, lcw_ref, lcb_ref, wg_ref, bg_ref, lam_ref,
                  wal_ref, bal_ref, gnorm_ref, scw_ref, scb_ref, dtb_ref, alog_ref,
                  dskip_ref, snorm_ref, e_ref, wout_ref, lng_ref, lnb_ref,
                  o_ref,
                  stage_l, stage_s, hcar, gla_st, ssd_st, o_scr, cat, *, alpha):
    first = pl.program_id(1) == 0
    x = x_ref[...]
    sh1 = mod_ref[0:1, :]
    sc1 = mod_ref[1:2, :]
    g1 = mod_ref[2:3, :]
    h = (x * (1.0 + sc1) + sh1).astype(BF16)

    @pl.when(first)
    def _():
        hcar[...] = jnp.zeros(hcar.shape, F32)
        gla_st[...] = jnp.zeros(gla_st.shape, F32)
        ssd_st[...] = jnp.zeros(ssd_st.shape, F32)

    row = lax.broadcasted_iota(jnp.int32, (TS, 1), 0)

    lx = _dot(h, win_ref[:, C_LRU_X:C_LRU_X + LRU_WIDTH])
    xc = _causal_conv(stage_l, lx, lcw_ref, lcb_ref, first)
    gts = _dot(xc.astype(BF16), wg_ref[...]) + bg_ref[...]
    r = _sigmoid(gts[:, :LRU_WIDTH])
    i_g = _sigmoid(gts[:, LRU_WIDTH:])
    log_a = (-LRU_C) * r * _softplus(-lam_ref[...])
    a = jnp.exp(log_a)
    u = jnp.sqrt(-jnp.tanh(log_a) * (a * a + 1.0)) * (i_g * xc)
    acc_a, acc_h = a, u
    d = 1
    while d < TS:
        sh_h = jnp.where(row >= d, pltpu.roll(acc_h, d, axis=0), 0.0)
        sh_a = jnp.where(row >= d, pltpu.roll(acc_a, d, axis=0), 1.0)
        acc_h = acc_h + acc_a * sh_h
        acc_a = acc_a * sh_a
        d *= 2
    h_lru = acc_h + acc_a * hcar[0:1, :]
    hcar[0:1, :] = h_lru[TS - 1:TS, :]
    lgate = _dot(h, win_ref[:, C_LRU_G:C_LRU_G + LRU_WIDTH])
    cat[:, 0:LRU_WIDTH] = (h_lru * _gelu_tanh(lgate)).astype(BF16)

    rd = _dot(h, win_ref[:, C_RD:C_RD + 128])

    q = _dot(h, win_ref[:, C_Q:C_Q + GLA_QK]) * (GLA_DK ** -0.5)
    k = _dot(h, win_ref[:, C_K:C_K + GLA_QK])
    v = _dot(h, win_ref[:, C_V:C_V + GLA_WIDTH]).astype(BF16)
    zal = _dot(rd.astype(BF16), wal_ref[...]) + bal_ref[...]
    log_alpha = (jnp.minimum(zal, 0.0) - jnp.log1p(jnp.exp(-jnp.abs(zal)))) * (1.0 / GLA_TAU)
    cum = _cumsum_rows(log_alpha, CHUNK)
    q_dec = q * jnp.exp(cum)
    k_dec = (k * jnp.exp(-cum)).astype(BF16)
    ri = lax.broadcasted_iota(jnp.int32, (TS, TS), 0)
    ci = lax.broadcasted_iota(jnp.int32, (TS, TS), 1)
    causal = ri >= ci
    blk_causal = causal & ((ri // CHUNK) == (ci // CHUNK))
    lane_head = lax.broadcasted_iota(jnp.int32, (1, GLA_QK), 1) // GLA_DK
    for hd in range(GLA_HEADS):
        qm = jnp.where(lane_head == hd, q_dec, 0.0).astype(BF16)
        att = jnp.where(blk_causal, _dot_nt(qm, k_dec), 0.0).astype(BF16)
        o_scr[:, hd * GLA_DV:(hd + 1) * GLA_DV] = _dot(att, v[:, hd * GLA_DV:(hd + 1) * GLA_DV])
    sr = lax.broadcasted_iota(jnp.int32, (GLA_WIDTH, GLA_QK), 0) // GLA_DV
    scol = lax.broadcasted_iota(jnp.int32, (GLA_WIDTH, GLA_QK), 1) // GLA_DK
    st_mask = sr == scol
    q_dec_b = q_dec.astype(BF16)
    for c in range(TS // CHUNK):
        rows = slice(c * CHUNK, (c + 1) * CHUNK)
        st = gla_st[...]
        o_scr[rows, :] = o_scr[rows, :] + _dot_nt(q_dec_b[rows, :], st.astype(BF16))
        cum_c = cum[rows, :]
        cum_last = cum_c[CHUNK - 1:CHUNK, :]
        k_end = (k[rows, :] * jnp.exp(cum_last - cum_c)).astype(BF16)
        upd = _dot_tn(v[rows, :], k_end)
        gla_st[...] = st * jnp.exp(cum_last) + jnp.where(st_mask, upd, 0.0)
    g_gate = _dot(h, win_ref[:, C_G:C_G + GLA_WIDTH])
    for hd in range(GLA_HEADS):
        cols = slice(hd * GLA_DV, (hd + 1) * GLA_DV)
        oh = o_scr[:, cols]
        ms = jnp.mean(oh * oh, axis=-1, keepdims=True)
        on = oh * lax.rsqrt(ms + RMS_EPS) * gnorm_ref[:, cols]
        cat[:, LRU_WIDTH + hd * GLA_DV:LRU_WIDTH + (hd + 1) * GLA_DV] = (
            on * _silu(g_gate[:, cols])).astype(BF16)

    xbc = _dot(h, win_ref[:, C_XBC:C_XBC + SSD_CONV_DIM])
    xbc = _silu(_causal_conv(stage_s, xbc, scw_ref, scb_ref, first))
    xs = xbc[:, :SSD_WIDTH]
    lane128 = lax.broadcasted_iota(jnp.int32, (1, 128), 1)
    dt_lane = (lane128 >= RD_DT0) & (lane128 < RD_DT0 + SSD_HEADS)
    a_neg = jnp.where(dt_lane, -jnp.exp(alog_ref[...]), 0.0)
    dtp = _softplus(rd + dtb_ref[...])
    a_cs = _cumsum_rows(dtp * a_neg, TS)
    a_last = a_cs[TS - 1:TS, :]
    ea = jnp.exp(a_cs)
    dte = jnp.exp(a_last - a_cs)
    expand = _dot(jnp.concatenate([dtp, dte, ea], axis=0).astype(BF16), e_ref[...])
    dt_x = expand[0:TS, :]
    dte_x = expand[TS:2 * TS, :]
    ea_x = expand[2 * TS:3 * TS, :]
    xdt = xs * dt_x
    xdt_b = xdt.astype(BF16)
    xdte_b = (xdt * dte_x).astype(BF16)
    a_cs_t = a_cs.T
    z = _dot(h, win_ref[:, C_Z:C_Z + SSD_WIDTH])
    lane_lo = lax.broadcasted_iota(jnp.int32, (1, 128), 1) < SSD_HEADDIM
    gw = SSD_HPG * SSD_HEADDIM
    for g in range(SSD_GROUPS):
        bm = xbc[:, SSD_WIDTH + g * SSD_STATE:SSD_WIDTH + (g + 1) * SSD_STATE].astype(BF16)
        cm = xbc[:, SSD_WIDTH + (SSD_GROUPS + g) * SSD_STATE:
                 SSD_WIDTH + (SSD_GROUPS + g + 1) * SSD_STATE].astype(BF16)
        cb = _dot_nt(cm, bm)
        st = ssd_st[g]
        y_off = _dot(cm, st.astype(BF16)) * ea_x[:, g * gw:(g + 1) * gw]
        yg = []
        for j in range(SSD_HPG // 2):
            ys = []
            for hh in (2 * j, 2 * j + 1):
                lane = RD_DT0 + g * SSD_HPG + hh
                diff = a_cs[:, lane:lane + 1] - a_cs_t[lane:lane + 1, :]
                m = (cb * jnp.exp(jnp.where(causal, diff, -jnp.inf))).astype(BF16)
                ys.append(_dot(m, xdt_b[:, g * gw + j * 128:g * gw + (j + 1) * 128]))
            cols = slice(g * gw + j * 128, g * gw + (j + 1) * 128)
            y = jnp.where(lane_lo, ys[0], ys[1]) + y_off[:, j * 128:(j + 1) * 128]
            y = y + xs[:, cols] * dskip_ref[:, cols]
            yg.append(y * _silu(z[:, cols]))
        ssd_st[g] = st * ea_x[TS - 1:TS, g * gw:(g + 1) * gw] + _dot_tn(bm, xdte_b[:, g * gw:(g + 1) * gw])
        ssq = yg[0] * yg[0]
        for t in yg[1:]:
            ssq = ssq + t * t
        ms = jnp.sum(ssq, axis=-1, keepdims=True) * (1.0 / gw)
        inv = lax.rsqrt(ms + RMS_EPS)
        for j, t in enumerate(yg):
            cols = slice(g * gw + j * 128, g * gw + (j + 1) * 128)
            cat[:, LRU_WIDTH + GLA_WIDTH + g * gw + j * 128:
                LRU_WIDTH + GLA_WIDTH + g * gw + (j + 1) * 128] = (
                    t * inv * snorm_ref[:, cols]).astype(BF16)

    mix = _dot(cat[...], wout_ref[...])
    o_ref[...] = _layer_norm(alpha * x + g1 * mix, lng_ref[...], lnb_ref[...])


def _full(shape):
    nd = len(shape)
    return pl.BlockSpec(shape, lambda *_: (0,) * nd)


def _mixer(x, mod_l, p, alpha):
    bsz, seq, d = x.shape
    small = [p["lcw"], p["lcb"], p["wg"], p["bg"], p["lam"], p["wal"], p["bal"], p["gnorm"],
             p["scw"], p["scb"], p["dtb"], p["alog"], p["dskip"], p["snorm"], p["e"],
             p["wout"], p["lng"], p["lnb"]]
    in_specs = [pl.BlockSpec((None, TS, d), lambda b, s: (b, s, 0)),
                pl.BlockSpec((None, 6, d), lambda b, s: (b, 0, 0)),
                _full(p["win"].shape)] + [_full(a.shape) for a in small]
    return pl.pallas_call(
        functools.partial(_mixer_kernel, alpha=alpha),
        grid=(bsz, seq // TS),
        in_specs=in_specs,
        out_specs=pl.BlockSpec((None, TS, d), lambda b, s: (b, s, 0)),
        out_shape=jax.ShapeDtypeStruct((bsz, seq, d), F32),
        scratch_shapes=[pltpu.VMEM((TS + 8, LRU_WIDTH), F32),
                        pltpu.VMEM((TS + 8, SSD_CONV_DIM), F32),
                        pltpu.VMEM((8, LRU_WIDTH), F32),
                        pltpu.VMEM((GLA_WIDTH, GLA_QK), F32),
                        pltpu.VMEM((SSD_GROUPS, SSD_STATE, SSD_HPG * SSD_HEADDIM), F32),
                        pltpu.VMEM((TS, GLA_WIDTH), F32),
                        pltpu.VMEM((TS, D_MIX), BF16)],
        compiler_params=pltpu.CompilerParams(
            dimension_semantics=("arbitrary", "arbitrary"), vmem_limit_bytes=VMEM_LIMIT),
        name="mixer",
    )(x, mod_l, p["win"], *small)


def _router_kernel(*refs):
    last = pl.program_id(0) == pl.num_programs(0) - 1

    @pl.when(jnp.logical_not(last))
    def _():
        _route_block(*refs)

    @pl.when(last)
    def _():
        for o_ref in refs[6:10]:
            o_ref[...] = jnp.zeros(o_ref.shape, o_ref.dtype)


def _route_block(x_ref, mod_ref, rwt_ref, rbias_ref, ustrict_ref, lstrict_ref,
                 xs_ref, pg_ref, nch_ref, lh_ref, msk_scr, sq_scr, p_scr):
    x = x_ref[...]
    sh2 = mod_ref[3:4, :]
    sc2 = mod_ref[4:5, :]
    h2 = x * (1.0 + sc2) + sh2
    logits = lax.dot_general(rwt_ref[...], h2, (((1,), (1,)), ((), ())),
                             preferred_element_type=F32, precision=lax.Precision.HIGHEST)
    scores = _sigmoid(logits)
    biased = scores + rbias_ref[...]

    row8 = lax.broadcasted_iota(jnp.int32, (EXPERTS_PER_GROUP, TB), 0)
    gsc = []
    for g in range(N_EXPERT_GROUPS):
        vals = biased[g * EXPERTS_PER_GROUP:(g + 1) * EXPERTS_PER_GROUP, :]
        m1 = jnp.max(vals, axis=0, keepdims=True)
        i1 = jnp.min(jnp.where(vals == m1, row8, EXPERTS_PER_GROUP), axis=0, keepdims=True)
        m2 = jnp.max(jnp.where(row8 == i1, -jnp.inf, vals), axis=0, keepdims=True)
        gsc.append(m1 + m2)
    for g in range(N_EXPERT_GROUPS):
        beaten = jnp.zeros((1, TB), F32)
        for o in range(N_EXPERT_GROUPS):
            if o == g:
                continue
            wins = (gsc[o] >= gsc[g]) if o < g else (gsc[o] > gsc[g])
            beaten = beaten + jnp.where(wins, 1.0, 0.0)
        keep = beaten < TOPK_GROUPS
        vals = biased[g * EXPERTS_PER_GROUP:(g + 1) * EXPERTS_PER_GROUP, :]
        msk_scr[g * EXPERTS_PER_GROUP:(g + 1) * EXPERTS_PER_GROUP, :] = jnp.where(keep, vals, -jnp.inf)

    cur = msk_scr[...]
    row64 = lax.broadcasted_iota(jnp.int32, (N_EXPERTS, TB), 0)
    sel = jnp.zeros((N_EXPERTS, TB), F32)
    for _ in range(TOP_K):
        m = jnp.max(cur, axis=0, keepdims=True)
        ik = jnp.min(jnp.where(cur == m, row64, N_EXPERTS), axis=0, keepdims=True)
        hit = row64 == ik
        sel = jnp.where(hit, 1.0, sel)
        cur = jnp.where(hit, -jnp.inf, cur)
    wsel = sel * scores
    gates = wsel * (ROUTE_SCALE / jnp.sum(wsel, axis=0, keepdims=True))

    rank = _dot(sel.astype(BF16), ustrict_ref[...])
    cnt = jnp.sum(sel, axis=1, keepdims=True)
    nch = jnp.floor((cnt + (ROW_CHUNK - 1.0)) * (1.0 / ROW_CHUNK))
    nch_b = jnp.broadcast_to(nch, (N_EXPERTS, TB))
    choff = _dot(lstrict_ref[...], nch_b.astype(BF16))
    lo = choff * float(ROW_CHUNK)
    posm = jnp.where(sel > 0.0, lo + rank, -1.0)
    nch_ref[...] = nch_b[:, 0:128]

    sq_scr[0:N_EXPERTS, :] = lo[:, 0:128]
    sq_scr[N_EXPERTS:2 * N_EXPERTS, :] = (lo + nch_b * float(ROW_CHUNK))[:, 0:128]
    lh_ref[...] = sq_scr[...].T[0:8, :]

    pg_ref[0:N_EXPERTS, :] = posm
    pg_ref[N_EXPERTS:2 * N_EXPERTS, :] = posm
    pg_ref[2 * N_EXPERTS:3 * N_EXPERTS, :] = gates
    pg_ref[3 * N_EXPERTS:4 * N_EXPERTS, :] = gates

    _sorted_row_matrix(pg_ref, lh_ref, p_scr, weighted=False)
    h2b = h2.astype(BF16)
    for n in range(D_MODEL // 256):
        cols = slice(n * 256, (n + 1) * 256)
        xs_ref[:, cols] = _dot(p_scr[...], h2b[:, cols]).astype(BF16)


def _sorted_row_matrix(pg_ref, lh_ref, out_scr, weighted):
    posm2 = pg_ref[0:2 * N_EXPERTS, :]
    pos_hi = jnp.floor(posm2 * (1.0 / 256.0))
    pos_lo1 = posm2 - pos_hi * 256.0 + 1.0
    lh_row = lh_ref[0:1, :]
    sign = jnp.where(lax.broadcasted_iota(jnp.int32, (1, 2 * N_EXPERTS), 1) < N_EXPERTS, 1.0, -1.0)
    r_l = lax.broadcasted_iota(jnp.int32, (256, 2 * N_EXPERTS), 0).astype(F32)
    r_t1 = lax.broadcasted_iota(jnp.int32, (256, TB), 0).astype(F32) + 1.0
    if weighted:
        gates2 = pg_ref[2 * N_EXPERTS:4 * N_EXPERTS, :].astype(BF16)
    for grp in range(N_GRP):
        owner = jnp.where(r_l + float(grp * 256) >= lh_row, sign, 0.0).astype(BF16)
        low = jnp.where(pos_hi == float(grp), pos_lo1, 0.0).astype(BF16)
        hit = _dot(owner, low) == r_t1
        val = _dot(owner, gates2) if weighted else 1.0
        out_scr[grp * 256:(grp + 1) * 256, :] = jnp.where(hit, val, 0.0).astype(BF16)


def _router(x1, mod_l, rwt, rbias, ustrict, lstrict, blocks_per_batch):
    t, d = x1.shape
    nb = t // TB
    return pl.pallas_call(
        _router_kernel,
        grid=(nb + 1,),
        in_specs=[pl.BlockSpec((TB, d), lambda i: (jnp.minimum(i, nb - 1), 0)),
                  pl.BlockSpec((None, 6, d),
                               lambda i: (jnp.minimum(i, nb - 1) // blocks_per_batch, 0, 0)),
                  _full(rwt.shape), _full(rbias.shape), _full(ustrict.shape), _full(lstrict.shape)],
        out_specs=[pl.BlockSpec((R_BLK, d), lambda i: (i, 0)),
                   pl.BlockSpec((None, 4 * N_EXPERTS, TB), lambda i: (i, 0, 0)),
                   pl.BlockSpec((None, N_EXPERTS, 128), lambda i: (i, 0, 0)),
                   pl.BlockSpec((None, 8, 128), lambda i: (i, 0, 0))],
        out_shape=[jax.ShapeDtypeStruct(((nb + 1) * R_BLK, d), BF16),
                   jax.ShapeDtypeStruct((nb + 1, 4 * N_EXPERTS, TB), F32),
                   jax.ShapeDtypeStruct((nb + 1, N_EXPERTS, 128), F32),
                   jax.ShapeDtypeStruct((nb + 1, 8, 128), F32)],
        scratch_shapes=[pltpu.VMEM((N_EXPERTS, TB), F32), pltpu.VMEM((128, 128), F32),
                        pltpu.VMEM((R_BLK, TB), BF16)],
        compiler_params=pltpu.CompilerParams(
            dimension_semantics=("arbitrary",), vmem_limit_bytes=VMEM_LIMIT),
        name="router",
    )(x1, mod_l, rwt, rbias, ustrict, lstrict)


def _expert_kernel(te_ref, src_ref, dst_ref, nt_ref, xs_hbm, w1_ref, w3_ref, w2_ref, ys_hbm,
                   xbuf, ybuf, w1b, w3b, w2b, sem_in, sem_out):
    j = pl.program_id(0)
    nt = nt_ref[0]
    slot = lax.rem(j, NBUF)

    def chunk_rows(cid):
        return pl.ds(pl.multiple_of(cid * ROW_CHUNK, ROW_CHUNK), ROW_CHUNK)

    def in_copy(cid, sl, i):
        return pltpu.make_async_copy(xs_hbm.at[chunk_rows(cid), :],
                                     xbuf.at[sl, pl.ds(i * ROW_CHUNK, ROW_CHUNK), :],
                                     sem_in.at[sl])

    def out_copy(cid, sl, i):
        return pltpu.make_async_copy(ybuf.at[sl, pl.ds(i * ROW_CHUNK, ROW_CHUNK), :],
                                     ys_hbm.at[chunk_rows(cid), :],
                                     sem_out.at[sl])

    def start_in(tile, sl):
        for i in range(CPT):
            in_copy(src_ref[tile * CPT + i], sl, i).start()

    def wait_in(sl):
        for i in range(CPT):
            in_copy(0, sl, i).wait()

    def start_out(tile, sl):
        for i in range(CPT):
            out_copy(dst_ref[tile * CPT + i], sl, i).start()

    def wait_out(sl):
        for i in range(CPT):
            out_copy(0, sl, i).wait()

    @pl.when(j == 0)
    def _():
        for t in range(NBUF - 1):
            @pl.when(t < nt)
            def _():
                start_in(t, t)

    ahead = j + (NBUF - 1)

    @pl.when(ahead < nt)
    def _():
        start_in(ahead, lax.rem(ahead, NBUF))

    @pl.when(j < nt)
    def _():
        wait_in(slot)
        prev = te_ref[jnp.maximum(j - 1, 0)]

        @pl.when((j == 0) | (te_ref[j] != prev))
        def _():
            w1b[...] = w1_ref[...].astype(BF16)
            w3b[...] = w3_ref[...].astype(BF16)
            w2b[...] = w2_ref[...].astype(BF16)

        @pl.when(j >= NBUF)
        def _():
            wait_out(slot)

        xt = xbuf[slot]
        hid = (_silu(_dot(xt, w1b[...])) * _dot(xt, w3b[...])).astype(BF16)
        ybuf[slot] = _dot(hid, w2b[...]).astype(BF16)
        start_out(j, slot)

        @pl.when(j == nt - 1)
        def _():
            for back in range(NBUF):
                @pl.when(j >= back)
                def _():
                    wait_out(lax.rem(j - back, NBUF))


def _experts(xs, te, src, dst, nt, w1, w3, w2, n_tiles_max):
    d = xs.shape[1]
    grid_spec = pltpu.PrefetchScalarGridSpec(
        num_scalar_prefetch=4,
        grid=(n_tiles_max,),
        in_specs=[pl.BlockSpec(memory_space=pl.ANY),
                  pl.BlockSpec((None, d, D_EXPERT), lambda j, te, *_: (te[j], 0, 0)),
                  pl.BlockSpec((None, d, D_EXPERT), lambda j, te, *_: (te[j], 0, 0)),
                  pl.BlockSpec((None, D_EXPERT, d), lambda j, te, *_: (te[j], 0, 0))],
        out_specs=pl.BlockSpec(memory_space=pl.ANY),
        scratch_shapes=[pltpu.VMEM((NBUF, TILE_M, d), BF16),
                        pltpu.VMEM((NBUF, TILE_M, d), BF16),
                        pltpu.VMEM((d, D_EXPERT), BF16),
                        pltpu.VMEM((d, D_EXPERT), BF16),
                        pltpu.VMEM((D_EXPERT, d), BF16),
                        pltpu.SemaphoreType.DMA((NBUF,)),
                        pltpu.SemaphoreType.DMA((NBUF,))])
    return pl.pallas_call(
        _expert_kernel,
        grid_spec=grid_spec,
        out_shape=jax.ShapeDtypeStruct(xs.shape, xs.dtype),
        input_output_aliases={4: 0},
        compiler_params=pltpu.CompilerParams(
            dimension_semantics=("arbitrary",), vmem_limit_bytes=VMEM_LIMIT),
        name="experts",
    )(te, src, dst, nt, xs, w1, w3, w2)


def _tile_plan(nch_f, n_tiles_max):
    nch = nch_f[:-1, :, 0]
    nb = nch.shape[0]
    hi = lax.Precision.HIGHEST
    choff = jnp.cumsum(nch, axis=1) - nch
    per_expert = jnp.sum(nch, axis=0)
    ntile = jnp.floor((per_expert + (CPT - 1.0)) * (1.0 / CPT))
    tile_end = jnp.cumsum(ntile)
    tile_start = tile_end - ntile
    nt = tile_end[-1]
    jt = jnp.arange(n_tiles_max, dtype=F32)
    te = jnp.minimum(jnp.sum((tile_end[None, :] <= jt[:, None]).astype(F32), axis=1), N_EXPERTS - 1.0)
    oh = (te[:, None] == jnp.arange(N_EXPERTS, dtype=F32)[None, :]).astype(F32)
    cum_t = jnp.dot(oh, jnp.cumsum(nch, axis=0).T, precision=hi)
    choff_t = jnp.dot(oh, choff.T, precision=hi)
    start_t = jnp.dot(oh, tile_start, precision=hi)
    total_t = jnp.dot(oh, per_expert, precision=hi)
    q = (jt - start_t)[:, None] * CPT + jnp.arange(CPT, dtype=F32)[None, :]
    le = cum_t[:, None, :] <= q[:, :, None]
    b = jnp.minimum(jnp.sum(le.astype(F32), axis=2), nb - 1.0)
    before = jnp.max(jnp.where(le, cum_t[:, None, :], 0.0), axis=2)
    is_b = jnp.arange(nb, dtype=F32)[None, None, :] == b[:, :, None]
    choff_s = jnp.sum(jnp.where(is_b, choff_t[:, None, :], 0.0), axis=2)
    valid = (jt[:, None] < nt) & (q < total_t[:, None])
    cid = b * (R_BLK // ROW_CHUNK) + choff_s + (q - before)
    assert (NBUF + 1) * CPT <= R_BLK // ROW_CHUNK
    lane = jnp.arange(CPT, dtype=F32)[None, :]
    spare = nb * (R_BLK // ROW_CHUNK) + jnp.mod(jt, float(NBUF))[:, None] * CPT + lane
    zero_chunk = nb * (R_BLK // ROW_CHUNK) + NBUF * CPT + lane
    src = jnp.where(valid, cid, zero_chunk).astype(jnp.int32).reshape(-1)
    dst = jnp.where(valid, cid, spare).astype(jnp.int32).reshape(-1)
    return te.astype(jnp.int32), src, dst, nt.astype(jnp.int32).reshape(1)


def _combine_kernel(x_ref, mod_ref, pg_ref, lh_ref, ys_ref, sw1_ref, sw3_ref, sw2_ref, lng_ref,
                    lnb_ref, o_ref, w_scr, *, alpha):
    x = x_ref[...]
    sh2 = mod_ref[3:4, :]
    sc2 = mod_ref[4:5, :]
    g2 = mod_ref[5:6, :]
    h2b = (x * (1.0 + sc2) + sh2).astype(BF16)
    hid = (_silu(_dot(h2b, sw1_ref[...])) * _dot(h2b, sw3_ref[...])).astype(BF16)
    ffn = _dot(hid, sw2_ref[...])
    _sorted_row_matrix(pg_ref, lh_ref, w_scr, weighted=True)
    ffn = ffn + _dot_tn(ys_ref[...], w_scr[...]).T
    o_ref[...] = _layer_norm(alpha * x + g2 * ffn, lng_ref[...], lnb_ref[...])


def _combine(x1, mod_l, pg, lh, ys, sw1, sw3, sw2, lng, lnb, blocks_per_batch, alpha):
    t, d = x1.shape
    nb = t // TB
    return pl.pallas_call(
        functools.partial(_combine_kernel, alpha=alpha),
        grid=(nb,),
        in_specs=[pl.BlockSpec((TB, d), lambda i: (i, 0)),
                  pl.BlockSpec((None, 6, d), lambda i: (i // blocks_per_batch, 0, 0)),
                  pl.BlockSpec((None, 4 * N_EXPERTS, TB), lambda i: (i, 0, 0)),
                  pl.BlockSpec((None, 8, 128), lambda i: (i, 0, 0)),
                  pl.BlockSpec((R_BLK, d), lambda i: (i, 0)),
                  _full(sw1.shape), _full(sw3.shape), _full(sw2.shape),
                  _full(lng.shape), _full(lnb.shape)],
        out_specs=pl.BlockSpec((TB, d), lambda i: (i, 0)),
        out_shape=jax.ShapeDtypeStruct((t, d), F32),
        scratch_shapes=[pltpu.VMEM((R_BLK, TB), BF16)],
        compiler_params=pltpu.CompilerParams(
            dimension_semantics=("arbitrary",), vmem_limit_bytes=VMEM_LIMIT),
        name="combine",
    )(x1, mod_l, pg, lh, ys, sw1, sw3, sw2, lng, lnb)


def kernel(x, c, w_ada, b_ada, w_in, lru_conv_w, lru_conv_b, lru_w_a, lru_b_a, lru_w_x, lru_b_x, lru_lambda, gla_w_alpha, gla_b_alpha, gla_norm, ssd_conv_w, ssd_conv_b, ssd_dt_bias, ssd_a_log, ssd_d, ssd_norm, w_out, ln1_g, ln1_b, router_w, router_bias, exp_w1, exp_w3, exp_w2, shared_w1, shared_w3, shared_w2, ln2_g, ln2_b):
    bsz, seq, d = x.shape
    n_layers = w_ada.shape[0]
    assert d == D_MODEL and seq % TS == 0 and seq % TB == 0
    alpha = float((2 * n_layers) ** 0.25)
    t = bsz * seq
    nb = t // TB
    blocks_per_batch = seq // TB
    n_tiles_max = -(-(nb * (R_BLK // ROW_CHUNK)) // CPT) + N_EXPERTS

    mod = _adaln(c, w_ada, b_ada).reshape(n_layers, bsz, 6, d)

    win = _relayout_w_in(w_in)
    eye = jnp.eye(LRU_BLOCKS, dtype=F32)
    bd = lambda w: jnp.einsum("lnij,nm->lnimj", w, eye).reshape(n_layers, LRU_WIDTH, LRU_WIDTH)
    wg = jnp.concatenate([bd(lru_w_a), bd(lru_w_x)], axis=-1).astype(BF16)
    bg = jnp.concatenate([lru_b_a, lru_b_x], axis=-1)[:, None, :]
    wal = jnp.concatenate([gla_w_alpha, jnp.zeros((n_layers, 128 - GLA_RANK, GLA_QK), F32)],
                          axis=1).astype(BF16)
    pad_dt = lambda v: jnp.pad(v, ((0, 0), (RD_DT0, 128 - RD_DT0 - SSD_HEADS)))[:, None, :]
    e_np = np.zeros((128, SSD_WIDTH), np.float32)
    for hh in range(SSD_HEADS):
        e_np[RD_DT0 + hh, hh * SSD_HEADDIM:(hh + 1) * SSD_HEADDIM] = 1.0
    e_mat = jnp.asarray(e_np, BF16)
    wout = w_out.astype(BF16)
    rwt = jnp.swapaxes(router_w, 1, 2)
    ustrict = jnp.asarray(np.triu(np.ones((TB, TB), np.float32), 1), BF16)
    lstrict = jnp.asarray(np.tril(np.ones((N_EXPERTS, N_EXPERTS), np.float32), -1), BF16)
    sw1 = shared_w1.astype(BF16)
    sw3 = shared_w3.astype(BF16)
    sw2 = shared_w2.astype(BF16)

    for l in range(n_layers):
        p = dict(win=win[l], lcw=lru_conv_w[l], lcb=lru_conv_b[l][None, :], wg=wg[l], bg=bg[l],
                 lam=lru_lambda[l][None, :], wal=wal[l], bal=gla_b_alpha[l][None, :],
                 gnorm=jnp.tile(gla_norm[l], GLA_HEADS)[None, :],
                 scw=ssd_conv_w[l], scb=ssd_conv_b[l][None, :],
                 dtb=pad_dt(ssd_dt_bias)[l], alog=pad_dt(ssd_a_log)[l],
                 dskip=jnp.repeat(ssd_d[l], SSD_HEADDIM)[None, :], snorm=ssd_norm[l][None, :],
                 e=e_mat, wout=wout[l], lng=ln1_g[l][None, :], lnb=ln1_b[l][None, :])
        x1 = _mixer(x, mod[l], p, alpha).reshape(t, d)
        xs, pg, nch, lh = _router(x1, mod[l], rwt[l], router_bias[l][:, None], ustrict, lstrict,
                                  blocks_per_batch)
        te, src, dst, nt = _tile_plan(nch, n_tiles_max)
        ys = _experts(xs, te, src, dst, nt, exp_w1[l], exp_w3[l], exp_w2[l], n_tiles_max)
        x = _combine(x1, mod[l], pg, lh, ys, sw1[l], sw3[l], sw2[l], ln2_g[l][None, :],
                     ln2_b[l][None, :], blocks_per_batch, alpha).reshape(bsz, seq, d)
    return x
```

```python
import functools

import jax
import jax.numpy as jnp
import numpy as np
from jax import lax
from jax.experimental import pallas as pl
from jax.experimental.pallas import tpu as pltpu

F32 = jnp.float32
BF16 = jnp.bfloat16

D_MODEL = 1024
CHUNK = 64
LRU_WIDTH = 512
LRU_BLOCKS = 8
LRU_C = 8.0
CONV_W = 4
GLA_WIDTH = 512
GLA_HEADS = 4
GLA_DV = 128
GLA_DK = 64
GLA_QK = 256
GLA_RANK = 16
GLA_TAU = 16.0
SSD_WIDTH = 1024
SSD_HEADDIM = 64
SSD_HEADS = 16
SSD_GROUPS = 2
SSD_HPG = 8
SSD_STATE = 128
SSD_CONV_DIM = SSD_WIDTH + 2 * SSD_GROUPS * SSD_STATE
D_MIX = 2048
N_EXPERTS = 64
TOP_K = 8
N_EXPERT_GROUPS = 8
EXPERTS_PER_GROUP = 8
TOPK_GROUPS = 4
D_EXPERT = 256
ROUTE_SCALE = 2.5
LN_EPS = 1e-5
RMS_EPS = 1e-6

C_LRU_X = 0
C_LRU_G = 512
C_Q = 1024
C_K = 1280
C_V = 1536
C_G = 2048
C_Z = 2560
C_XBC = 3584
C_RD = 5120
D_IN_PAD = 5248
RD_DT0 = 16

TS = 256
MIX_ROWS = 2
TB = 256
ROW_CHUNK = 16
CPT = 32
TILE_M = ROW_CHUNK * CPT
NBUF = 4
R_BLK = -(-(TOP_K * TB + N_EXPERTS * (ROW_CHUNK - 1)) // 256) * 256
N_GRP = R_BLK // 256
VMEM_LIMIT = 56 * 1024 * 1024


def _dot(a, b):
    return jnp.dot(a, b, preferred_element_type=F32)


def _dot_nt(a, b):
    return lax.dot_general(a, b, (((1,), (1,)), ((), ())), preferred_element_type=F32)


def _dot_tn(a, b):
    return lax.dot_general(a, b, (((0,), (0,)), ((), ())), preferred_element_type=F32)


def _sigmoid(x):
    return jax.nn.sigmoid(x)


def _silu(x):
    return x * _sigmoid(x)


def _softplus(x):
    return jnp.maximum(x, 0.0) + jnp.log1p(jnp.exp(-jnp.abs(x)))


def _gelu_tanh(x):
    return 0.5 * x * (1.0 + jnp.tanh(0.7978845608028654 * (x + 0.044715 * (x * x * x))))


def _layer_norm(x, g, b):
    mu = jnp.mean(x, axis=-1, keepdims=True)
    xc = x - mu
    var = jnp.mean(xc * xc, axis=-1, keepdims=True)
    return xc * lax.rsqrt(var + LN_EPS) * g + b


def _shift_rows(x, d, fill, rowmod):
    return jnp.where(rowmod >= d, pltpu.roll(x, d, axis=0), fill)


def _cumsum_rows(x, period):
    rowmod = lax.broadcasted_iota(jnp.int32, x.shape, 0) & (period - 1)
    d = 1
    while d < period:
        x = x + _shift_rows(x, d, 0.0, rowmod)
        d *= 2
    return x


def _causal_conv(stage_ref, x, w_ref, b_ref):
    n = x.shape[0]
    stage_ref[8:8 + n, :] = x
    y = b_ref[...] + w_ref[CONV_W - 1:CONV_W, :] * x
    for k in range(CONV_W - 1):
        y = y + w_ref[k:k + 1, :] * stage_ref[5 + k:5 + k + n, :]
    stage_ref[0:8, :] = stage_ref[n:n + 8, :]
    return y


def _adaln_kernel(c_ref, w_ref, b_ref, o_ref):
    cond = _silu(c_ref[...])
    o_ref[...] = jnp.dot(cond, w_ref[...], preferred_element_type=F32,
                         precision=lax.Precision.HIGHEST) + b_ref[...]


def _adaln(c, w_ada, b_ada):
    n_layers, d, n = w_ada.shape
    bsz = c.shape[0]
    tn = 1024
    return pl.pallas_call(
        _adaln_kernel,
        grid=(n_layers, n // tn),
        in_specs=[pl.BlockSpec((bsz, d), lambda l, j: (0, 0)),
                  pl.BlockSpec((None, d, tn), lambda l, j: (l, 0, j)),
                  pl.BlockSpec((None, 1, tn), lambda l, j: (l, 0, j))],
        out_specs=pl.BlockSpec((None, bsz, tn), lambda l, j: (l, 0, j)),
        out_shape=jax.ShapeDtypeStruct((n_layers, bsz, n), F32),
        name="adaln",
    )(c, w_ada, b_ada.reshape(n_layers, 1, n))


O_R = 2 * LRU_WIDTH + 2 * GLA_QK + 2 * GLA_WIDTH
O_Z = O_R + GLA_RANK
O_DT = O_Z + SSD_WIDTH + SSD_CONV_DIM
D_IN_PROJ = O_DT + SSD_HEADS


def _relayout_kernel(w_ref, o_ref):
    o_ref[:, 0:O_R] = w_ref[:, 0:O_R].astype(BF16)
    o_ref[:, C_Z:C_RD] = w_ref[:, O_Z:O_DT].astype(BF16)
    tail = jnp.concatenate(
        [w_ref[:, O_R:O_Z], w_ref[:, O_DT:D_IN_PROJ],
         jnp.zeros((w_ref.shape[0], 128 - GLA_RANK - SSD_HEADS), F32)], axis=1)
    o_ref[:, C_RD:D_IN_PAD] = tail.astype(BF16)


def _relayout_w_in(w_in):
    n_layers, d, n = w_in.shape
    assert n == D_IN_PROJ and O_R == C_Z and C_RD - C_Z == O_DT - O_Z
    rows = 256
    return pl.pallas_call(
        _relayout_kernel,
        grid=(n_layers, d // rows),
        in_specs=[pl.BlockSpec((None, rows, n), lambda l, i: (l, i, 0))],
        out_specs=pl.BlockSpec((None, rows, D_IN_PAD), lambda l, i: (l, i, 0)),
        out_shape=jax.ShapeDtypeStruct((n_layers, d, D_IN_PAD), BF16),
        name="relayout_w_in",
    )(w_in)


def _mixer_kernel(x_ref, mod_ref, *refs, alpha):
    weights, o_ref, scratch = refs[:19], refs[19], refs[20:]

    @pl.when(pl.program_id(1) == 0)
    def _():
        for s_ref in scratch[:5]:
            s_ref[...] = jnp.zeros(s_ref.shape, F32)

    for c in range(MIX_ROWS):
        _mixer_tile(x_ref.at[c], mod_ref.at[c], *weights, o_ref.at[c],
                    *[s_ref.at[c] for s_ref in scratch], alpha=alpha)


def _mixer_tile(x_ref, mod_ref, win_ref, lcw_ref, lcb_ref, wg_ref, bg_ref, lam_ref,
                wal_ref, bal_ref, gnorm_ref, scw_ref, scb_ref, dtb_ref, alog_ref,
                dskip_ref, snorm_ref, e_ref, wout_ref, lng_ref, lnb_ref,
                o_ref,
                stage_l, stage_s, hcar, gla_st, ssd_st, o_scr, cat, *, alpha):
    x = x_ref[...]
    sh1 = mod_ref[0:1, :]
    sc1 = mod_ref[1:2, :]
    g1 = mod_ref[2:3, :]
    h = (x * (1.0 + sc1) + sh1).astype(BF16)
    row = lax.broadcasted_iota(jnp.int32, (TS, 1), 0)

    lx = _dot(h, win_ref[:, C_LRU_X:C_LRU_X + LRU_WIDTH])
    xc = _causal_conv(stage_l, lx, lcw_ref, lcb_ref)
    gts = _dot(xc.astype(BF16), wg_ref[...]) + bg_ref[...]
    r = _sigmoid(gts[:, :LRU_WIDTH])
    i_g = _sigmoid(gts[:, LRU_WIDTH:])
    log_a = (-LRU_C) * r * _softplus(-lam_ref[...])
    a = jnp.exp(log_a)
    u = jnp.sqrt(-jnp.tanh(log_a) * (a * a + 1.0)) * (i_g * xc)
    acc_a, acc_h = a, u
    d = 1
    while d < TS:
        sh_h = jnp.where(row >= d, pltpu.roll(acc_h, d, axis=0), 0.0)
        sh_a = jnp.where(row >= d, pltpu.roll(acc_a, d, axis=0), 1.0)
        acc_h = acc_h + acc_a * sh_h
        acc_a = acc_a * sh_a
        d *= 2
    h_lru = acc_h + acc_a * hcar[0:1, :]
    hcar[0:1, :] = h_lru[TS - 1:TS, :]
    lgate = _dot(h, win_ref[:, C_LRU_G:C_LRU_G + LRU_WIDTH])
    cat[:, 0:LRU_WIDTH] = (h_lru * _gelu_tanh(lgate)).astype(BF16)

    rd = _dot(h, win_ref[:, C_RD:C_RD + 128])

    q = _dot(h, win_ref[:, C_Q:C_Q + GLA_QK]) * (GLA_DK ** -0.5)
    k = _dot(h, win_ref[:, C_K:C_K + GLA_QK])
    v = _dot(h, win_ref[:, C_V:C_V + GLA_WIDTH]).astype(BF16)
    zal = _dot(rd.astype(BF16), wal_ref[...]) + bal_ref[...]
    log_alpha = (jnp.minimum(zal, 0.0) - jnp.log1p(jnp.exp(-jnp.abs(zal)))) * (1.0 / GLA_TAU)
    cum = _cumsum_rows(log_alpha, CHUNK)
    q_dec = q * jnp.exp(cum)
    k_dec = (k * jnp.exp(-cum)).astype(BF16)
    ri = lax.broadcasted_iota(jnp.int32, (TS, TS), 0)
    ci = lax.broadcasted_iota(jnp.int32, (TS, TS), 1)
    causal = ri >= ci
    blk_causal = causal & ((ri // CHUNK) == (ci // CHUNK))
    lane_head = lax.broadcasted_iota(jnp.int32, (1, GLA_QK), 1) // GLA_DK
    for hd in range(GLA_HEADS):
        qm = jnp.where(lane_head == hd, q_dec, 0.0).astype(BF16)
        att = jnp.where(blk_causal, _dot_nt(qm, k_dec), 0.0).astype(BF16)
        o_scr[:, hd * GLA_DV:(hd + 1) * GLA_DV] = _dot(att, v[:, hd * GLA_DV:(hd + 1) * GLA_DV])
    sr = lax.broadcasted_iota(jnp.int32, (GLA_WIDTH, GLA_QK), 0) // GLA_DV
    scol = lax.broadcasted_iota(jnp.int32, (GLA_WIDTH, GLA_QK), 1) // GLA_DK
    st_mask = sr == scol
    q_dec_b = q_dec.astype(BF16)
    for c in range(TS // CHUNK):
        rows = slice(c * CHUNK, (c + 1) * CHUNK)
        st = gla_st[...]
        o_scr[rows, :] = o_scr[rows, :] + _dot_nt(q_dec_b[rows, :], st.astype(BF16))
        cum_c = cum[rows, :]
        cum_last = cum_c[CHUNK - 1:CHUNK, :]
        k_end = (k[rows, :] * jnp.exp(cum_last - cum_c)).astype(BF16)
        upd = _dot_tn(v[rows, :], k_end)
        gla_st[...] = st * jnp.exp(cum_last) + jnp.where(st_mask, upd, 0.0)
    g_gate = _dot(h, win_ref[:, C_G:C_G + GLA_WIDTH])
    for hd in range(GLA_HEADS):
        cols = slice(hd * GLA_DV, (hd + 1) * GLA_DV)
        oh = o_scr[:, cols]
        ms = jnp.mean(oh * oh, axis=-1, keepdims=True)
        on = oh * lax.rsqrt(ms + RMS_EPS) * gnorm_ref[:, cols]
        cat[:, LRU_WIDTH + hd * GLA_DV:LRU_WIDTH + (hd + 1) * GLA_DV] = (
            on * _silu(g_gate[:, cols])).astype(BF16)

    xbc = _dot(h, win_ref[:, C_XBC:C_XBC + SSD_CONV_DIM])
    xbc = _silu(_causal_conv(stage_s, xbc, scw_ref, scb_ref))
    xs = xbc[:, :SSD_WIDTH]
    lane128 = lax.broadcasted_iota(jnp.int32, (1, 128), 1)
    dt_lane = (lane128 >= RD_DT0) & (lane128 < RD_DT0 + SSD_HEADS)
    a_neg = jnp.where(dt_lane, -jnp.exp(alog_ref[...]), 0.0)
    dtp = _softplus(rd + dtb_ref[...])
    a_cs = _cumsum_rows(dtp * a_neg, TS)
    a_last = a_cs[TS - 1:TS, :]
    ea = jnp.exp(a_cs)
    dte = jnp.exp(a_last - a_cs)
    expand = _dot(jnp.concatenate([dtp, dte, ea], axis=0).astype(BF16), e_ref[...])
    dt_x = expand[0:TS, :]
    dte_x = expand[TS:2 * TS, :]
    ea_x = expand[2 * TS:3 * TS, :]
    xdt = xs * dt_x
    xdt_b = xdt.astype(BF16)
    xdte_b = (xdt * dte_x).astype(BF16)
    a_cs_t = a_cs.T
    z = _dot(h, win_ref[:, C_Z:C_Z + SSD_WIDTH])
    lane_lo = lax.broadcasted_iota(jnp.int32, (1, 128), 1) < SSD_HEADDIM
    gw = SSD_HPG * SSD_HEADDIM
    for g in range(SSD_GROUPS):
        bm = xbc[:, SSD_WIDTH + g * SSD_STATE:SSD_WIDTH + (g + 1) * SSD_STATE].astype(BF16)
        cm = xbc[:, SSD_WIDTH + (SSD_GROUPS + g) * SSD_STATE:
                 SSD_WIDTH + (SSD_GROUPS + g + 1) * SSD_STATE].astype(BF16)
        cb = _dot_nt(cm, bm)
        st = ssd_st[g]
        y_off = _dot(cm, st.astype(BF16)) * ea_x[:, g * gw:(g + 1) * gw]
        yg = []
        for j in range(SSD_HPG // 2):
            ys = []
            for hh in (2 * j, 2 * j + 1):
                lane = RD_DT0 + g * SSD_HPG + hh
                diff = a_cs[:, lane:lane + 1] - a_cs_t[lane:lane + 1, :]
                m = (cb * jnp.exp(jnp.where(causal, diff, -jnp.inf))).astype(BF16)
                ys.append(_dot(m, xdt_b[:, g * gw + j * 128:g * gw + (j + 1) * 128]))
            cols = slice(g * gw + j * 128, g * gw + (j + 1) * 128)
            y = jnp.where(lane_lo, ys[0], ys[1]) + y_off[:, j * 128:(j + 1) * 128]
            y = y + xs[:, cols] * dskip_ref[:, cols]
            yg.append(y * _silu(z[:, cols]))
        ssd_st[g] = st * ea_x[TS - 1:TS, g * gw:(g + 1) * gw] + _dot_tn(bm, xdte_b[:, g * gw:(g + 1) * gw])
        ssq = yg[0] * yg[0]
        for t in yg[1:]:
            ssq = ssq + t * t
        ms = jnp.sum(ssq, axis=-1, keepdims=True) * (1.0 / gw)
        inv = lax.rsqrt(ms + RMS_EPS)
        for j, t in enumerate(yg):
            cols = slice(g * gw + j * 128, g * gw + (j + 1) * 128)
            cat[:, LRU_WIDTH + GLA_WIDTH + g * gw + j * 128:
                LRU_WIDTH + GLA_WIDTH + g * gw + (j + 1) * 128] = (
                    t * inv * snorm_ref[:, cols]).astype(BF16)

    mix = _dot(cat[...], wout_ref[...])
    o_ref[...] = _layer_norm(alpha * x + g1 * mix, lng_ref[...], lnb_ref[...])


def _full(shape):
    nd = len(shape)
    return pl.BlockSpec(shape, lambda *_: (0,) * nd)


def _layer(arr, l):
    nd = arr.ndim
    return pl.BlockSpec((None,) + arr.shape[1:], lambda *_: (l,) + (0,) * (nd - 1))


def _mixer(x, mod, p, alpha, l):
    bsz, seq, d = x.shape
    names = ["win", "lcw", "lcb", "wg", "bg", "lam", "wal", "bal", "gnorm", "scw", "scb", "dtb",
             "alog", "dskip", "snorm", "e", "wout", "lng", "lnb"]
    weights = [p[n] for n in names]
    assert bsz % MIX_ROWS == 0
    in_specs = [pl.BlockSpec((MIX_ROWS, TS, d), lambda b, s: (b, s, 0)),
                pl.BlockSpec((None, MIX_ROWS, 6, d), lambda b, s: (l, b, 0, 0))]
    in_specs += [_full(w.shape) if n == "e" else _layer(w, l) for n, w in zip(names, weights)]
    return pl.pallas_call(
        functools.partial(_mixer_kernel, alpha=alpha),
        grid=(bsz // MIX_ROWS, seq // TS),
        in_specs=in_specs,
        out_specs=pl.BlockSpec((MIX_ROWS, TS, d), lambda b, s: (b, s, 0)),
        out_shape=jax.ShapeDtypeStruct((bsz, seq, d), F32),
        scratch_shapes=[pltpu.VMEM((MIX_ROWS, TS + 8, LRU_WIDTH), F32),
                        pltpu.VMEM((MIX_ROWS, TS + 8, SSD_CONV_DIM), F32),
                        pltpu.VMEM((MIX_ROWS, 8, LRU_WIDTH), F32),
                        pltpu.VMEM((MIX_ROWS, GLA_WIDTH, GLA_QK), F32),
                        pltpu.VMEM((MIX_ROWS, SSD_GROUPS, SSD_STATE, SSD_HPG * SSD_HEADDIM), F32),
                        pltpu.VMEM((MIX_ROWS, TS, GLA_WIDTH), F32),
                        pltpu.VMEM((MIX_ROWS, TS, D_MIX), BF16)],
        compiler_params=pltpu.CompilerParams(
            dimension_semantics=("arbitrary", "arbitrary"), vmem_limit_bytes=VMEM_LIMIT),
        name="mixer",
    )(x, mod, *weights)


def _router_kernel(*refs):
    last = pl.program_id(0) == pl.num_programs(0) - 1

    @pl.when(jnp.logical_not(last))
    def _():
        _route_block(*refs)

    @pl.when(last)
    def _():
        for o_ref in refs[6:10]:
            o_ref[...] = jnp.zeros(o_ref.shape, o_ref.dtype)


def _route_block(x_ref, mod_ref, rwt_ref, rbias_ref, ustrict_ref, lstrict_ref,
                 xs_ref, pg_ref, nch_ref, lh_ref, msk_scr, sq_scr, p_scr):
    x = x_ref[...]
    sh2 = mod_ref[3:4, :]
    sc2 = mod_ref[4:5, :]
    h2 = x * (1.0 + sc2) + sh2
    logits = lax.dot_general(rwt_ref[...], h2, (((1,), (1,)), ((), ())),
                             preferred_element_type=F32, precision=lax.Precision.HIGHEST)
    scores = _sigmoid(logits)
    biased = scores + rbias_ref[...]

    row8 = lax.broadcasted_iota(jnp.int32, (EXPERTS_PER_GROUP, TB), 0)
    gsc = []
    for g in range(N_EXPERT_GROUPS):
        vals = biased[g * EXPERTS_PER_GROUP:(g + 1) * EXPERTS_PER_GROUP, :]
        m1 = jnp.max(vals, axis=0, keepdims=True)
        i1 = jnp.min(jnp.where(vals == m1, row8, EXPERTS_PER_GROUP), axis=0, keepdims=True)
        m2 = jnp.max(jnp.where(row8 == i1, -jnp.inf, vals), axis=0, keepdims=True)
        gsc.append(m1 + m2)
    for g in range(N_EXPERT_GROUPS):
        beaten = jnp.zeros((1, TB), F32)
        for o in range(N_EXPERT_GROUPS):
            if o == g:
                continue
            wins = (gsc[o] >= gsc[g]) if o < g else (gsc[o] > gsc[g])
            beaten = beaten + jnp.where(wins, 1.0, 0.0)
        keep = beaten < TOPK_GROUPS
        vals = biased[g * EXPERTS_PER_GROUP:(g + 1) * EXPERTS_PER_GROUP, :]
        msk_scr[g * EXPERTS_PER_GROUP:(g + 1) * EXPERTS_PER_GROUP, :] = jnp.where(keep, vals, -jnp.inf)

    cur = msk_scr[...]
    row64 = lax.broadcasted_iota(jnp.int32, (N_EXPERTS, TB), 0)
    sel = jnp.zeros((N_EXPERTS, TB), F32)
    for _ in range(TOP_K):
        m = jnp.max(cur, axis=0, keepdims=True)
        ik = jnp.min(jnp.where(cur == m, row64, N_EXPERTS), axis=0, keepdims=True)
        hit = row64 == ik
        sel = jnp.where(hit, 1.0, sel)
        cur = jnp.where(hit, -jnp.inf, cur)
    wsel = sel * scores
    gates = wsel * (ROUTE_SCALE / jnp.sum(wsel, axis=0, keepdims=True))

    rank = _dot(sel.astype(BF16), ustrict_ref[...])
    cnt = jnp.sum(sel, axis=1, keepdims=True)
    nch = jnp.floor((cnt + (ROW_CHUNK - 1.0)) * (1.0 / ROW_CHUNK))
    nch_b = jnp.broadcast_to(nch, (N_EXPERTS, TB))
    choff = _dot(lstrict_ref[...], nch_b.astype(BF16))
    lo = choff * float(ROW_CHUNK)
    posm = jnp.where(sel > 0.0, lo + rank, -1.0)
    nch_ref[...] = nch_b[:, 0:128]

    sq_scr[0:N_EXPERTS, :] = lo[:, 0:128]
    sq_scr[N_EXPERTS:2 * N_EXPERTS, :] = (lo + nch_b * float(ROW_CHUNK))[:, 0:128]
    lh_ref[...] = sq_scr[...].T[0:8, :]

    pg_ref[0:N_EXPERTS, :] = posm
    pg_ref[N_EXPERTS:2 * N_EXPERTS, :] = posm
    pg_ref[2 * N_EXPERTS:3 * N_EXPERTS, :] = gates
    pg_ref[3 * N_EXPERTS:4 * N_EXPERTS, :] = gates

    _sorted_row_matrix(pg_ref, lh_ref, p_scr, weighted=False)
    h2b = h2.astype(BF16)
    for n in range(D_MODEL // 256):
        cols = slice(n * 256, (n + 1) * 256)
        xs_ref[:, cols] = _dot(p_scr[...], h2b[:, cols]).astype(BF16)


def _sorted_row_matrix(pg_ref, lh_ref, out_scr, weighted):
    posm2 = pg_ref[0:2 * N_EXPERTS, :]
    pos_hi = jnp.floor(posm2 * (1.0 / 256.0))
    pos_lo1 = posm2 - pos_hi * 256.0 + 1.0
    lh_row = lh_ref[0:1, :]
    sign = jnp.where(lax.broadcasted_iota(jnp.int32, (1, 2 * N_EXPERTS), 1) < N_EXPERTS, 1.0, -1.0)
    r_l = lax.broadcasted_iota(jnp.int32, (256, 2 * N_EXPERTS), 0).astype(F32)
    r_t1 = lax.broadcasted_iota(jnp.int32, (256, TB), 0).astype(F32) + 1.0
    if weighted:
        gates2 = pg_ref[2 * N_EXPERTS:4 * N_EXPERTS, :].astype(BF16)
    for grp in range(N_GRP):
        owner = jnp.where(r_l + float(grp * 256) >= lh_row, sign, 0.0).astype(BF16)
        low = jnp.where(pos_hi == float(grp), pos_lo1, 0.0).astype(BF16)
        hit = _dot(owner, low) == r_t1
        val = _dot(owner, gates2) if weighted else 1.0
        out_scr[grp * 256:(grp + 1) * 256, :] = jnp.where(hit, val, 0.0).astype(BF16)


def _router(x1, mod, rwt, rbias, ustrict, lstrict, blocks_per_batch, l):
    t, d = x1.shape
    nb = t // TB
    return pl.pallas_call(
        _router_kernel,
        grid=(nb + 1,),
        in_specs=[pl.BlockSpec((TB, d), lambda i: (jnp.minimum(i, nb - 1), 0)),
                  pl.BlockSpec((None, None, 6, d),
                               lambda i: (l, jnp.minimum(i, nb - 1) // blocks_per_batch, 0, 0)),
                  _layer(rwt, l), _layer(rbias, l), _full(ustrict.shape), _full(lstrict.shape)],
        out_specs=[pl.BlockSpec((R_BLK, d), lambda i: (i, 0)),
                   pl.BlockSpec((None, 4 * N_EXPERTS, TB), lambda i: (i, 0, 0)),
                   pl.BlockSpec((None, N_EXPERTS, 128), lambda i: (i, 0, 0)),
                   pl.BlockSpec((None, 8, 128), lambda i: (i, 0, 0))],
        out_shape=[jax.ShapeDtypeStruct(((nb + 1) * R_BLK, d), BF16),
                   jax.ShapeDtypeStruct((nb + 1, 4 * N_EXPERTS, TB), F32),
                   jax.ShapeDtypeStruct((nb + 1, N_EXPERTS, 128), F32),
                   jax.ShapeDtypeStruct((nb + 1, 8, 128), F32)],
        scratch_shapes=[pltpu.VMEM((N_EXPERTS, TB), F32), pltpu.VMEM((128, 128), F32),
                        pltpu.VMEM((R_BLK, TB), BF16)],
        compiler_params=pltpu.CompilerParams(
            dimension_semantics=("arbitrary",), vmem_limit_bytes=VMEM_LIMIT),
        name="router",
    )(x1, mod, rwt, rbias, ustrict, lstrict)


def _expert_kernel(te_ref, src_ref, dst_ref, nt_ref, xs_hbm, w1_ref, w3_ref, w2_ref, ys_hbm,
                   xbuf, ybuf, w1b, w3b, w2b, sem_in, sem_out):
    j = pl.program_id(0)
    nt = nt_ref[0]
    slot = lax.rem(j, NBUF)

    def in_copy(cid, sl, i):
        return pltpu.make_async_copy(xs_hbm.at[cid], xbuf.at[sl, i], sem_in.at[sl])

    def out_copy(cid, sl, i):
        return pltpu.make_async_copy(ybuf.at[sl, i], ys_hbm.at[cid], sem_out.at[sl])

    def start_in(tile, sl):
        for i in range(CPT):
            in_copy(src_ref[tile * CPT + i], sl, i).start()

    def wait_in(sl):
        for i in range(CPT):
            in_copy(0, sl, i).wait()

    def start_out(tile, sl):
        for i in range(CPT):
            out_copy(dst_ref[(tile + 1) * CPT + i], sl, i).start()

    def wait_out(sl):
        for i in range(CPT):
            out_copy(0, sl, i).wait()

    @pl.when(j == 0)
    def _():
        ybuf[NBUF - 1] = jnp.zeros(ybuf.shape[1:], BF16)
        for t in range(NBUF - 1):
            start_in(t, t)

    @pl.when(j < nt)
    def _():
        wait_in(slot)
        prev = te_ref[jnp.maximum(j - 1, 0)]

        @pl.when((j == 0) | (te_ref[j] != prev))
        def _():
            w1b[...] = w1_ref[...].astype(BF16)
            w3b[...] = w3_ref[...].astype(BF16)
            w2b[...] = w2_ref[...].astype(BF16)

        @pl.when(j >= NBUF - 1)
        def _():
            wait_out(slot)

        xt = xbuf[slot].reshape(TILE_M, D_MODEL)
        gate = _dot(xt, w1b[...])
        up = _dot(xt, w3b[...])
        ahead = j + (NBUF - 1)
        start_in(ahead, lax.rem(ahead, NBUF))
        start_out(j - 1, lax.rem(ahead, NBUF))
        hid = (_silu(gate) * up).astype(BF16)
        ybuf[slot] = _dot(hid, w2b[...]).astype(BF16).reshape(CPT, ROW_CHUNK, D_MODEL)

        @pl.when(j == nt - 1)
        def _():
            start_out(j, slot)
            for back in range(NBUF):
                @pl.when(j + 1 >= back)
                def _():
                    wait_out(lax.rem(j + NBUF - back, NBUF))
            for fwd in range(1, NBUF):
                wait_in(lax.rem(j + fwd, NBUF))


def _experts(xs, te, src, dst, nt, w1, w3, w2, n_tiles_max, l):
    rows, d = xs.shape
    xs = xs.reshape(rows // ROW_CHUNK, ROW_CHUNK, d)
    grid_spec = pltpu.PrefetchScalarGridSpec(
        num_scalar_prefetch=4,
        grid=(n_tiles_max,),
        in_specs=[pl.BlockSpec(memory_space=pl.ANY),
                  pl.BlockSpec((None, None, d, D_EXPERT), lambda j, te, *_: (l, te[j], 0, 0)),
                  pl.BlockSpec((None, None, d, D_EXPERT), lambda j, te, *_: (l, te[j], 0, 0)),
                  pl.BlockSpec((None, None, D_EXPERT, d), lambda j, te, *_: (l, te[j], 0, 0))],
        out_specs=pl.BlockSpec(memory_space=pl.ANY),
        scratch_shapes=[pltpu.VMEM((NBUF, CPT, ROW_CHUNK, d), BF16),
                        pltpu.VMEM((NBUF, CPT, ROW_CHUNK, d), BF16),
                        pltpu.VMEM((d, D_EXPERT), BF16),
                        pltpu.VMEM((d, D_EXPERT), BF16),
                        pltpu.VMEM((D_EXPERT, d), BF16),
                        pltpu.SemaphoreType.DMA((NBUF,)),
                        pltpu.SemaphoreType.DMA((NBUF,))])
    return pl.pallas_call(
        _expert_kernel,
        grid_spec=grid_spec,
        out_shape=jax.ShapeDtypeStruct(xs.shape, xs.dtype),
        input_output_aliases={4: 0},
        compiler_params=pltpu.CompilerParams(
            dimension_semantics=("arbitrary",), vmem_limit_bytes=VMEM_LIMIT),
        name="experts",
    )(te, src, dst, nt, xs, w1, w3, w2).reshape(rows, d)


def _tile_plan(nch_f, n_tiles_max):
    nch = nch_f[:-1, :, 0]
    nb = nch.shape[0]
    hi = lax.Precision.HIGHEST
    choff = jnp.cumsum(nch, axis=1) - nch
    per_expert = jnp.sum(nch, axis=0)
    ntile = jnp.floor((per_expert + (CPT - 1.0)) * (1.0 / CPT))
    tile_end = jnp.cumsum(ntile)
    tile_start = tile_end - ntile
    nt = tile_end[-1]
    jt = jnp.arange(n_tiles_max, dtype=F32)
    te = jnp.minimum(jnp.sum((tile_end[None, :] <= jt[:, None]).astype(F32), axis=1), N_EXPERTS - 1.0)
    oh = (te[:, None] == jnp.arange(N_EXPERTS, dtype=F32)[None, :]).astype(F32)
    cum_t = jnp.dot(oh, jnp.cumsum(nch, axis=0).T, precision=hi)
    choff_t = jnp.dot(oh, choff.T, precision=hi)
    start_t = jnp.dot(oh, tile_start, precision=hi)
    total_t = jnp.dot(oh, per_expert, precision=hi)
    q = (jt - start_t)[:, None] * CPT + jnp.arange(CPT, dtype=F32)[None, :]
    le = cum_t[:, None, :] <= q[:, :, None]
    b = jnp.minimum(jnp.sum(le.astype(F32), axis=2), nb - 1.0)
    before = jnp.max(jnp.where(le, cum_t[:, None, :], 0.0), axis=2)
    is_b = jnp.arange(nb, dtype=F32)[None, None, :] == b[:, :, None]
    choff_s = jnp.sum(jnp.where(is_b, choff_t[:, None, :], 0.0), axis=2)
    valid = (jt[:, None] < nt) & (q < total_t[:, None])
    cid = b * (R_BLK // ROW_CHUNK) + choff_s + (q - before)
    assert (NBUF + 1) * CPT <= R_BLK // ROW_CHUNK
    lane = jnp.arange(CPT, dtype=F32)[None, :]
    spare0 = nb * (R_BLK // ROW_CHUNK)
    spare = spare0 + jnp.mod(jt, float(NBUF))[:, None] * CPT + lane
    zero_chunk = spare0 + NBUF * CPT + lane
    src = jnp.where(valid, cid, zero_chunk).astype(jnp.int32).reshape(-1)
    dst = jnp.concatenate([spare0 + (NBUF - 1) * CPT + lane, jnp.where(valid, cid, spare)],
                          axis=0).astype(jnp.int32).reshape(-1)
    return te.astype(jnp.int32), src, dst, nt.astype(jnp.int32).reshape(1)


def _combine_kernel(x_ref, mod_ref, pg_ref, lh_ref, ys_ref, sw1_ref, sw3_ref, sw2_ref, lng_ref,
                    lnb_ref, o_ref, w_scr, *, alpha):
    x = x_ref[...]
    sh2 = mod_ref[3:4, :]
    sc2 = mod_ref[4:5, :]
    g2 = mod_ref[5:6, :]
    h2b = (x * (1.0 + sc2) + sh2).astype(BF16)
    hid = (_silu(_dot(h2b, sw1_ref[...])) * _dot(h2b, sw3_ref[...])).astype(BF16)
    ffn = _dot(hid, sw2_ref[...])
    _sorted_row_matrix(pg_ref, lh_ref, w_scr, weighted=True)
    ffn = ffn + _dot_tn(ys_ref[...], w_scr[...]).T
    o_ref[...] = _layer_norm(alpha * x + g2 * ffn, lng_ref[...], lnb_ref[...])


def _combine(x1, mod, pg, lh, ys, sw1, sw3, sw2, lng, lnb, blocks_per_batch, alpha, l):
    t, d = x1.shape
    nb = t // TB
    return pl.pallas_call(
        functools.partial(_combine_kernel, alpha=alpha),
        grid=(nb,),
        in_specs=[pl.BlockSpec((TB, d), lambda i: (i, 0)),
                  pl.BlockSpec((None, None, 6, d), lambda i: (l, i // blocks_per_batch, 0, 0)),
                  pl.BlockSpec((None, 4 * N_EXPERTS, TB), lambda i: (i, 0, 0)),
                  pl.BlockSpec((None, 8, 128), lambda i: (i, 0, 0)),
                  pl.BlockSpec((R_BLK, d), lambda i: (i, 0)),
                  _layer(sw1, l), _layer(sw3, l), _layer(sw2, l), _layer(lng, l), _layer(lnb, l)],
        out_specs=pl.BlockSpec((TB, d), lambda i: (i, 0)),
        out_shape=jax.ShapeDtypeStruct((t, d), F32),
        scratch_shapes=[pltpu.VMEM((R_BLK, TB), BF16)],
        compiler_params=pltpu.CompilerParams(
            dimension_semantics=("arbitrary",), vmem_limit_bytes=VMEM_LIMIT),
        name="combine",
    )(x1, mod, pg, lh, ys, sw1, sw3, sw2, lng, lnb)


def kernel(x, c, w_ada, b_ada, w_in, lru_conv_w, lru_conv_b, lru_w_a, lru_b_a, lru_w_x, lru_b_x, lru_lambda, gla_w_alpha, gla_b_alpha, gla_norm, ssd_conv_w, ssd_conv_b, ssd_dt_bias, ssd_a_log, ssd_d, ssd_norm, w_out, ln1_g, ln1_b, router_w, router_bias, exp_w1, exp_w3, exp_w2, shared_w1, shared_w3, shared_w2, ln2_g, ln2_b):
    bsz, seq, d = x.shape
    n_layers = w_ada.shape[0]
    assert d == D_MODEL and seq % TS == 0 and seq % TB == 0
    alpha = float((2 * n_layers) ** 0.25)
    t = bsz * seq
    nb = t // TB
    blocks_per_batch = seq // TB
    n_tiles_max = -(-(nb * (R_BLK // ROW_CHUNK)) // CPT) + N_EXPERTS

    mod = _adaln(c, w_ada, b_ada).reshape(n_layers, bsz, 6, d)

    win = _relayout_w_in(w_in)
    eye = jnp.eye(LRU_BLOCKS, dtype=F32)
    bd = lambda w: jnp.einsum("lnij,nm->lnimj", w, eye).reshape(n_layers, LRU_WIDTH, LRU_WIDTH)
    wg = jnp.concatenate([bd(lru_w_a), bd(lru_w_x)], axis=-1).astype(BF16)
    bg = jnp.concatenate([lru_b_a, lru_b_x], axis=-1)[:, None, :]
    wal = jnp.concatenate([gla_w_alpha, jnp.zeros((n_layers, 128 - GLA_RANK, GLA_QK), F32)],
                          axis=1).astype(BF16)
    pad_dt = lambda v: jnp.pad(v, ((0, 0), (RD_DT0, 128 - RD_DT0 - SSD_HEADS)))[:, None, :]
    e_np = np.zeros((128, SSD_WIDTH), np.float32)
    for hh in range(SSD_HEADS):
        e_np[RD_DT0 + hh, hh * SSD_HEADDIM:(hh + 1) * SSD_HEADDIM] = 1.0
    e_mat = jnp.asarray(e_np, BF16)
    wout = w_out.astype(BF16)
    rwt = jnp.swapaxes(router_w, 1, 2)
    ustrict = jnp.asarray(np.triu(np.ones((TB, TB), np.float32), 1), BF16)
    lstrict = jnp.asarray(np.tril(np.ones((N_EXPERTS, N_EXPERTS), np.float32), -1), BF16)
    sw1 = shared_w1.astype(BF16)
    sw3 = shared_w3.astype(BF16)
    sw2 = shared_w2.astype(BF16)

    row = lambda v: v[:, None, :]
    p = dict(win=win, lcw=lru_conv_w, lcb=row(lru_conv_b), wg=wg, bg=bg, lam=row(lru_lambda),
             wal=wal, bal=row(gla_b_alpha), gnorm=row(jnp.tile(gla_norm, (1, GLA_HEADS))),
             scw=ssd_conv_w, scb=row(ssd_conv_b), dtb=pad_dt(ssd_dt_bias), alog=pad_dt(ssd_a_log),
             dskip=row(jnp.repeat(ssd_d, SSD_HEADDIM, axis=1)), snorm=row(ssd_norm),
             e=e_mat, wout=wout, lng=row(ln1_g), lnb=row(ln1_b))
    rbias = router_bias[:, :, None]
    ln2g, ln2b = row(ln2_g), row(ln2_b)

    for l in range(n_layers):
        x1 = _mixer(x, mod, p, alpha, l).reshape(t, d)
        xs, pg, nch, lh = _router(x1, mod, rwt, rbias, ustrict, lstrict, blocks_per_batch, l)
        te, src, dst, nt = _tile_plan(nch, n_tiles_max + NBUF - 1)
        ys = _experts(xs, te, src, dst, nt, exp_w1, exp_w3, exp_w2, n_tiles_max, l)
        x = _combine(x1, mod, pg, lh, ys, sw1, sw3, sw2, ln2g, ln2b, blocks_per_batch, alpha,
                     l).reshape(bsz, seq, d)
    return x
```

```python
import functools

import jax
import jax.numpy as jnp
import numpy as np
from jax import lax
from jax.experimental import pallas as pl
from jax.experimental.pallas import tpu as pltpu

F32 = jnp.float32
BF16 = jnp.bfloat16

D_MODEL = 1024
CHUNK = 64
LRU_WIDTH = 512
LRU_BLOCKS = 8
LRU_C = 8.0
CONV_W = 4
GLA_WIDTH = 512
GLA_HEADS = 4
GLA_DV = 128
GLA_DK = 64
GLA_QK = 256
GLA_RANK = 16
GLA_TAU = 16.0
SSD_WIDTH = 1024
SSD_HEADDIM = 64
SSD_HEADS = 16
SSD_GROUPS = 2
SSD_HPG = 8
SSD_STATE = 128
SSD_CONV_DIM = SSD_WIDTH + 2 * SSD_GROUPS * SSD_STATE
D_MIX = 2048
N_EXPERTS = 64
TOP_K = 8
N_EXPERT_GROUPS = 8
EXPERTS_PER_GROUP = 8
TOPK_GROUPS = 4
D_EXPERT = 256
ROUTE_SCALE = 2.5
LN_EPS = 1e-5
RMS_EPS = 1e-6

C_LRU_X = 0
C_LRU_G = 512
C_Q = 1024
C_K = 1280
C_V = 1536
C_G = 2048
C_Z = 2560
C_XBC = 3584
C_RD = 5120
D_IN_PAD = 5248
RD_DT0 = 16

TS = 256
MIX_ROWS = 2
N_MIX_WEIGHTS = 20
TB = 256
RB_BLOCKS = 2
ROW_CHUNK = 16
CPT = 32
TILE_M = ROW_CHUNK * CPT
NBUF = 4
R_BLK = -(-(TOP_K * TB + N_EXPERTS * (ROW_CHUNK - 1)) // 256) * 256
N_GRP = R_BLK // 256
VMEM_LIMIT = 56 * 1024 * 1024


def _dot(a, b):
    return jnp.dot(a, b, preferred_element_type=F32)


def _dot_nt(a, b):
    return lax.dot_general(a, b, (((1,), (1,)), ((), ())), preferred_element_type=F32)


def _dot_tn(a, b):
    return lax.dot_general(a, b, (((0,), (0,)), ((), ())), preferred_element_type=F32)


def _sigmoid(x):
    return jax.nn.sigmoid(x)


def _silu(x):
    return x * _sigmoid(x)


def _softplus(x):
    return jnp.maximum(x, 0.0) + jnp.log1p(jnp.exp(-jnp.abs(x)))


def _gelu_tanh(x):
    return 0.5 * x * (1.0 + jnp.tanh(0.7978845608028654 * (x + 0.044715 * (x * x * x))))


def _layer_norm(x, g, b):
    mu = jnp.mean(x, axis=-1, keepdims=True)
    xc = x - mu
    var = jnp.mean(xc * xc, axis=-1, keepdims=True)
    return xc * lax.rsqrt(var + LN_EPS) * g + b


def _shift_rows(x, d, fill, rowmod):
    return jnp.where(rowmod >= d, pltpu.roll(x, d, axis=0), fill)


def _cumsum_rows(x, period):
    rowmod = lax.broadcasted_iota(jnp.int32, x.shape, 0) & (period - 1)
    d = 1
    while d < period:
        x = x + _shift_rows(x, d, 0.0, rowmod)
        d *= 2
    return x


def _causal_conv(carry_ref, x, w_ref, b_ref, shift_ref):
    n = x.shape[0]
    delayed = _dot(shift_ref[...], x.astype(BF16))
    y = b_ref[...] + w_ref[CONV_W - 1:CONV_W, :] * x
    carry = carry_ref[...]
    row8 = lax.broadcasted_iota(jnp.int32, (8, 1), 0)
    head = jnp.zeros((8, x.shape[1]), F32)
    for k in range(CONV_W - 1):
        delay = CONV_W - 1 - k
        y = y + w_ref[k:k + 1, :] * delayed[k * n:(k + 1) * n, :]
        head = head + w_ref[k:k + 1, :] * jnp.where(row8 < delay, pltpu.roll(carry, delay, axis=0), 0.0)
    carry_ref[...] = x[n - 8:n, :]
    return jnp.concatenate([y[0:8, :] + head, y[8:n, :]], axis=0)


def _adaln_kernel(c_ref, w_ref, b_ref, o_ref):
    cond = _silu(c_ref[...])
    o_ref[...] = jnp.dot(cond, w_ref[...], preferred_element_type=F32,
                         precision=lax.Precision.HIGHEST) + b_ref[...]


def _adaln(c, w_ada, b_ada):
    n_layers, d, n = w_ada.shape
    bsz = c.shape[0]
    tn = 1024
    return pl.pallas_call(
        _adaln_kernel,
        grid=(n_layers, n // tn),
        in_specs=[pl.BlockSpec((bsz, d), lambda l, j: (0, 0)),
                  pl.BlockSpec((None, d, tn), lambda l, j: (l, 0, j)),
                  pl.BlockSpec((None, 1, tn), lambda l, j: (l, 0, j))],
        out_specs=pl.BlockSpec((None, bsz, tn), lambda l, j: (l, 0, j)),
        out_shape=jax.ShapeDtypeStruct((n_layers, bsz, n), F32),
        name="adaln",
    )(c, w_ada, b_ada.reshape(n_layers, 1, n))


O_R = 2 * LRU_WIDTH + 2 * GLA_QK + 2 * GLA_WIDTH
O_Z = O_R + GLA_RANK
O_DT = O_Z + SSD_WIDTH + SSD_CONV_DIM
D_IN_PROJ = O_DT + SSD_HEADS


def _relayout_kernel(w_ref, o_ref):
    o_ref[:, 0:O_R] = w_ref[:, 0:O_R].astype(BF16)
    o_ref[:, C_Z:C_RD] = w_ref[:, O_Z:O_DT].astype(BF16)
    tail = jnp.concatenate(
        [w_ref[:, O_R:O_Z], w_ref[:, O_DT:D_IN_PROJ],
         jnp.zeros((w_ref.shape[0], 128 - GLA_RANK - SSD_HEADS), F32)], axis=1)
    o_ref[:, C_RD:D_IN_PAD] = tail.astype(BF16)


def _relayout_w_in(w_in):
    n_layers, d, n = w_in.shape
    assert n == D_IN_PROJ and O_R == C_Z and C_RD - C_Z == O_DT - O_Z
    rows = 256
    return pl.pallas_call(
        _relayout_kernel,
        grid=(n_layers, d // rows),
        in_specs=[pl.BlockSpec((None, rows, n), lambda l, i: (l, i, 0))],
        out_specs=pl.BlockSpec((None, rows, D_IN_PAD), lambda l, i: (l, i, 0)),
        out_shape=jax.ShapeDtypeStruct((n_layers, d, D_IN_PAD), BF16),
        name="relayout_w_in",
    )(w_in)


def _mixer_kernel(x_ref, mod_ref, *refs, alpha):
    weights, o_ref, scratch = refs[:N_MIX_WEIGHTS], refs[N_MIX_WEIGHTS], refs[N_MIX_WEIGHTS + 1:]

    @pl.when(pl.program_id(1) == 0)
    def _():
        for s_ref in scratch[:5]:
            s_ref[...] = jnp.zeros(s_ref.shape, F32)

    _mixer_tile(x_ref, mod_ref, *weights, o_ref, *scratch, alpha=alpha)


def _mixer_tile(x_ref, mod_ref, win_ref, lcw_ref, lcb_ref, wg_ref, bg_ref, lam_ref,
                wal_ref, bal_ref, gnorm_ref, scw_ref, scb_ref, dtb_ref, alog_ref,
                dskip_ref, snorm_ref, e_ref, shift_ref, wout_ref, lng_ref, lnb_ref,
                o_ref,
                stage_l, stage_s, hcar, gla_st, ssd_st, o_scr, cat, *, alpha):
    chains = range(MIX_ROWS)
    rows_of = lambda c: slice(c * TS, (c + 1) * TS)
    stack = lambda parts: jnp.concatenate(parts, axis=0)
    per_chain = lambda fn: stack([jnp.broadcast_to(fn(c), (TS, fn(c).shape[1])) for c in chains])

    x = x_ref[...].reshape(MIX_ROWS * TS, D_MODEL)
    sh1 = per_chain(lambda c: mod_ref[c, 0:1, :])
    sc1 = per_chain(lambda c: mod_ref[c, 1:2, :])
    g1 = per_chain(lambda c: mod_ref[c, 2:3, :])
    h = (x * (1.0 + sc1) + sh1).astype(BF16)
    row = lax.broadcasted_iota(jnp.int32, (MIX_ROWS * TS, 1), 0) & (TS - 1)

    lx = _dot(h, win_ref[:, C_LRU_X:C_LRU_X + LRU_WIDTH])
    xc = stack([_causal_conv(stage_l.at[c], lx[rows_of(c)], lcw_ref, lcb_ref, shift_ref)
                for c in chains])
    gts = _dot(xc.astype(BF16), wg_ref[...]) + bg_ref[...]
    r = _sigmoid(gts[:, :LRU_WIDTH])
    i_g = _sigmoid(gts[:, LRU_WIDTH:])
    log_a = (-LRU_C) * r * _softplus(-lam_ref[...])
    a = jnp.exp(log_a)
    u = jnp.sqrt(-jnp.tanh(log_a) * (a * a + 1.0)) * (i_g * xc)
    acc_a, acc_h = a, u
    d = 1
    while d < TS:
        sh_h = jnp.where(row >= d, pltpu.roll(acc_h, d, axis=0), 0.0)
        sh_a = jnp.where(row >= d, pltpu.roll(acc_a, d, axis=0), 1.0)
        acc_h = acc_h + acc_a * sh_h
        acc_a = acc_a * sh_a
        d *= 2
    h_lru = acc_h + acc_a * per_chain(lambda c: hcar[c, 0:1, :])
    for c in chains:
        hcar[c, 0:1, :] = h_lru[(c + 1) * TS - 1:(c + 1) * TS, :]
    lgate = _dot(h, win_ref[:, C_LRU_G:C_LRU_G + LRU_WIDTH])
    cat[:, 0:LRU_WIDTH] = (h_lru * _gelu_tanh(lgate)).astype(BF16)

    rd = _dot(h, win_ref[:, C_RD:C_RD + 128])

    q = _dot(h, win_ref[:, C_Q:C_Q + GLA_QK]) * (GLA_DK ** -0.5)
    k = _dot(h, win_ref[:, C_K:C_K + GLA_QK])
    v = _dot(h, win_ref[:, C_V:C_V + GLA_WIDTH]).astype(BF16)
    zal = _dot(rd.astype(BF16), wal_ref[...]) + bal_ref[...]
    log_alpha = (jnp.minimum(zal, 0.0) - jnp.log1p(jnp.exp(-jnp.abs(zal)))) * (1.0 / GLA_TAU)
    cum = _cumsum_rows(log_alpha, CHUNK)
    q_dec = q * jnp.exp(cum)
    k_dec = (k * jnp.exp(-cum)).astype(BF16)
    ri = lax.broadcasted_iota(jnp.int32, (TS, TS), 0)
    ci = lax.broadcasted_iota(jnp.int32, (TS, TS), 1)
    causal = ri >= ci
    blk_causal = causal & ((ri // CHUNK) == (ci // CHUNK))
    lane_head = lax.broadcasted_iota(jnp.int32, (1, GLA_QK), 1) // GLA_DK
    for hd in range(GLA_HEADS):
        qm = jnp.where(lane_head == hd, q_dec, 0.0).astype(BF16)
        for c in chains:
            att = jnp.where(blk_causal, _dot_nt(qm[rows_of(c)], k_dec[rows_of(c)]), 0.0)
            o_scr[rows_of(c), hd * GLA_DV:(hd + 1) * GLA_DV] = _dot(
                att.astype(BF16), v[rows_of(c), hd * GLA_DV:(hd + 1) * GLA_DV])
    sr = lax.broadcasted_iota(jnp.int32, (GLA_WIDTH, GLA_QK), 0) // GLA_DV
    scol = lax.broadcasted_iota(jnp.int32, (GLA_WIDTH, GLA_QK), 1) // GLA_DK
    st_mask = sr == scol
    q_dec_b = q_dec.astype(BF16)
    for c in chains:
        for ch in range(TS // CHUNK):
            rows = slice(c * TS + ch * CHUNK, c * TS + (ch + 1) * CHUNK)
            st = gla_st[c]
            o_scr[rows, :] = o_scr[rows, :] + _dot_nt(q_dec_b[rows, :], st.astype(BF16))
            cum_c = cum[rows, :]
            cum_last = cum_c[CHUNK - 1:CHUNK, :]
            k_end = (k[rows, :] * jnp.exp(cum_last - cum_c)).astype(BF16)
            upd = _dot_tn(v[rows, :], k_end)
            gla_st[c] = st * jnp.exp(cum_last) + jnp.where(st_mask, upd, 0.0)
    g_gate = _dot(h, win_ref[:, C_G:C_G + GLA_WIDTH])
    for hd in range(GLA_HEADS):
        cols = slice(hd * GLA_DV, (hd + 1) * GLA_DV)
        oh = o_scr[:, cols]
        ms = jnp.mean(oh * oh, axis=-1, keepdims=True)
        on = oh * lax.rsqrt(ms + RMS_EPS) * gnorm_ref[:, cols]
        cat[:, LRU_WIDTH + hd * GLA_DV:LRU_WIDTH + (hd + 1) * GLA_DV] = (
            on * _silu(g_gate[:, cols])).astype(BF16)

    xbc = _dot(h, win_ref[:, C_XBC:C_XBC + SSD_CONV_DIM])
    xbc = _silu(stack([_causal_conv(stage_s.at[c], xbc[rows_of(c)], scw_ref, scb_ref, shift_ref)
                       for c in chains]))
    xs = xbc[:, :SSD_WIDTH]
    lane128 = lax.broadcasted_iota(jnp.int32, (1, 128), 1)
    dt_lane = (lane128 >= RD_DT0) & (lane128 < RD_DT0 + SSD_HEADS)
    a_neg = jnp.where(dt_lane, -jnp.exp(alog_ref[...]), 0.0)
    dtp = _softplus(rd + dtb_ref[...])
    a_cs = _cumsum_rows(dtp * a_neg, TS)
    a_last = per_chain(lambda c: a_cs[(c + 1) * TS - 1:(c + 1) * TS, :])
    ea = jnp.exp(a_cs)
    dte = jnp.exp(a_last - a_cs)
    n_rows = MIX_ROWS * TS
    expand = _dot(jnp.concatenate([dtp, dte, ea], axis=0).astype(BF16), e_ref[...])
    dt_x = expand[0:n_rows, :]
    dte_x = expand[n_rows:2 * n_rows, :]
    ea_x = expand[2 * n_rows:3 * n_rows, :]
    xdt = xs * dt_x
    xdt_b = xdt.astype(BF16)
    xdte_b = (xdt * dte_x).astype(BF16)
    a_cs_t = a_cs.T
    z = _dot(h, win_ref[:, C_Z:C_Z + SSD_WIDTH])
    lane_lo = lax.broadcasted_iota(jnp.int32, (1, 128), 1) < SSD_HEADDIM
    gw = SSD_HPG * SSD_HEADDIM
    for g in range(SSD_GROUPS):
        gcols = slice(g * gw, (g + 1) * gw)
        bm = xbc[:, SSD_WIDTH + g * SSD_STATE:SSD_WIDTH + (g + 1) * SSD_STATE].astype(BF16)
        cm = xbc[:, SSD_WIDTH + (SSD_GROUPS + g) * SSD_STATE:
                 SSD_WIDTH + (SSD_GROUPS + g + 1) * SSD_STATE].astype(BF16)
        y_chain = []
        for c in chains:
            rc = rows_of(c)
            cb = _dot_nt(cm[rc], bm[rc])
            st = ssd_st[c, g]
            y_off = _dot(cm[rc], st.astype(BF16)) * ea_x[rc, gcols]
            y_pairs = []
            for j in range(SSD_HPG // 2):
                ys = []
                for hh in (2 * j, 2 * j + 1):
                    lane = RD_DT0 + g * SSD_HPG + hh
                    diff = a_cs[rc, lane:lane + 1] - a_cs_t[lane:lane + 1, rc]
                    m = (cb * jnp.exp(jnp.where(causal, diff, -jnp.inf))).astype(BF16)
                    ys.append(_dot(m, xdt_b[rc, g * gw + j * 128:g * gw + (j + 1) * 128]))
                y_pairs.append(jnp.where(lane_lo, ys[0], ys[1]) + y_off[:, j * 128:(j + 1) * 128])
            y_chain.append(y_pairs)
            ssd_st[c, g] = (st * ea_x[(c + 1) * TS - 1:(c + 1) * TS, gcols]
                            + _dot_tn(bm[rc], xdte_b[rc, gcols]))
        yg = []
        for j in range(SSD_HPG // 2):
            cols = slice(g * gw + j * 128, g * gw + (j + 1) * 128)
            y = stack([y_chain[c][j] for c in chains]) + xs[:, cols] * dskip_ref[:, cols]
            yg.append(y * _silu(z[:, cols]))
        ssq = yg[0] * yg[0]
        for t in yg[1:]:
            ssq = ssq + t * t
        ms = jnp.sum(ssq, axis=-1, keepdims=True) * (1.0 / gw)
        inv = lax.rsqrt(ms + RMS_EPS)
        for j, t in enumerate(yg):
            cols = slice(g * gw + j * 128, g * gw + (j + 1) * 128)
            cat[:, LRU_WIDTH + GLA_WIDTH + g * gw + j * 128:
                LRU_WIDTH + GLA_WIDTH + g * gw + (j + 1) * 128] = (
                    t * inv * snorm_ref[:, cols]).astype(BF16)

    mix = _dot(cat[...], wout_ref[...])
    out = _layer_norm(alpha * x + g1 * mix, lng_ref[...], lnb_ref[...])
    o_ref[...] = out.reshape(MIX_ROWS, TS, D_MODEL)


def _full(shape):
    nd = len(shape)
    return pl.BlockSpec(shape, lambda *_: (0,) * nd)


def _layer(arr, l):
    nd = arr.ndim
    return pl.BlockSpec((None,) + arr.shape[1:], lambda *_: (l,) + (0,) * (nd - 1))


def _mixer(x, mod, p, alpha, l):
    bsz, seq, d = x.shape
    names = ["win", "lcw", "lcb", "wg", "bg", "lam", "wal", "bal", "gnorm", "scw", "scb", "dtb",
             "alog", "dskip", "snorm", "e", "shift", "wout", "lng", "lnb"]
    weights = [p[n] for n in names]
    assert bsz % MIX_ROWS == 0 and len(names) == N_MIX_WEIGHTS
    in_specs = [pl.BlockSpec((MIX_ROWS, TS, d), lambda b, s: (b, s, 0)),
                pl.BlockSpec((None, MIX_ROWS, 6, d), lambda b, s: (l, b, 0, 0))]
    in_specs += [_full(w.shape) if n in ("e", "shift") else _layer(w, l)
                 for n, w in zip(names, weights)]
    return pl.pallas_call(
        functools.partial(_mixer_kernel, alpha=alpha),
        grid=(bsz // MIX_ROWS, seq // TS),
        in_specs=in_specs,
        out_specs=pl.BlockSpec((MIX_ROWS, TS, d), lambda b, s: (b, s, 0)),
        out_shape=jax.ShapeDtypeStruct((bsz, seq, d), F32),
        scratch_shapes=[pltpu.VMEM((MIX_ROWS, 8, LRU_WIDTH), F32),
                        pltpu.VMEM((MIX_ROWS, 8, SSD_CONV_DIM), F32),
                        pltpu.VMEM((MIX_ROWS, 8, LRU_WIDTH), F32),
                        pltpu.VMEM((MIX_ROWS, GLA_WIDTH, GLA_QK), F32),
                        pltpu.VMEM((MIX_ROWS, SSD_GROUPS, SSD_STATE, SSD_HPG * SSD_HEADDIM), F32),
                        pltpu.VMEM((MIX_ROWS * TS, GLA_WIDTH), F32),
                        pltpu.VMEM((MIX_ROWS * TS, D_MIX), BF16)],
        compiler_params=pltpu.CompilerParams(
            dimension_semantics=("arbitrary", "arbitrary"), vmem_limit_bytes=VMEM_LIMIT),
        name="mixer",
    )(x, mod, *weights)


def _router_kernel(*refs):
    last = pl.program_id(0) == pl.num_programs(0) - 1

    @pl.when(jnp.logical_not(last))
    def _():
        x_ref, mod_ref, rwt_ref, rbias_ref, ustrict_ref, lstrict_ref = refs[:6]
        xs_ref, pg_ref, nch_ref, lh_ref, msk_scr, sq_scr, p_scr = refs[6:]
        for c in range(RB_BLOCKS):
            _route_block(x_ref.at[pl.ds(c * TB, TB)], mod_ref, rwt_ref, rbias_ref, ustrict_ref,
                         lstrict_ref, xs_ref.at[pl.ds(c * R_BLK, R_BLK)], pg_ref.at[c],
                         nch_ref.at[c], lh_ref.at[c], msk_scr.at[c], sq_scr.at[c], p_scr.at[c])

    @pl.when(last)
    def _():
        for o_ref in refs[6:10]:
            o_ref[...] = jnp.zeros(o_ref.shape, o_ref.dtype)


def _route_block(x_ref, mod_ref, rwt_ref, rbias_ref, ustrict_ref, lstrict_ref,
                 xs_ref, pg_ref, nch_ref, lh_ref, msk_scr, sq_scr, p_scr):
    x = x_ref[...]
    sh2 = mod_ref[3:4, :]
    sc2 = mod_ref[4:5, :]
    h2 = x * (1.0 + sc2) + sh2
    logits = lax.dot_general(rwt_ref[...], h2, (((1,), (1,)), ((), ())),
                             preferred_element_type=F32, precision=lax.Precision.HIGHEST)
    scores = _sigmoid(logits)
    biased = scores + rbias_ref[...]

    row8 = lax.broadcasted_iota(jnp.int32, (EXPERTS_PER_GROUP, TB), 0)
    gsc = []
    for g in range(N_EXPERT_GROUPS):
        vals = biased[g * EXPERTS_PER_GROUP:(g + 1) * EXPERTS_PER_GROUP, :]
        m1 = jnp.max(vals, axis=0, keepdims=True)
        i1 = jnp.min(jnp.where(vals == m1, row8, EXPERTS_PER_GROUP), axis=0, keepdims=True)
        m2 = jnp.max(jnp.where(row8 == i1, -jnp.inf, vals), axis=0, keepdims=True)
        gsc.append(m1 + m2)
    for g in range(N_EXPERT_GROUPS):
        beaten = jnp.zeros((1, TB), F32)
        for o in range(N_EXPERT_GROUPS):
            if o == g:
                continue
            wins = (gsc[o] >= gsc[g]) if o < g else (gsc[o] > gsc[g])
            beaten = beaten + jnp.where(wins, 1.0, 0.0)
        keep = beaten < TOPK_GROUPS
        vals = biased[g * EXPERTS_PER_GROUP:(g + 1) * EXPERTS_PER_GROUP, :]
        msk_scr[g * EXPERTS_PER_GROUP:(g + 1) * EXPERTS_PER_GROUP, :] = jnp.where(keep, vals, -jnp.inf)

    cur = msk_scr[...]
    row64 = lax.broadcasted_iota(jnp.int32, (N_EXPERTS, TB), 0)
    sel = jnp.zeros((N_EXPERTS, TB), F32)
    for _ in range(TOP_K):
        m = jnp.max(cur, axis=0, keepdims=True)
        ik = jnp.min(jnp.where(cur == m, row64, N_EXPERTS), axis=0, keepdims=True)
        hit = row64 == ik
        sel = jnp.where(hit, 1.0, sel)
        cur = jnp.where(hit, -jnp.inf, cur)
    wsel = sel * scores
    gates = wsel * (ROUTE_SCALE / jnp.sum(wsel, axis=0, keepdims=True))

    rank = _dot(sel.astype(BF16), ustrict_ref[...])
    cnt = jnp.sum(sel, axis=1, keepdims=True)
    nch = jnp.floor((cnt + (ROW_CHUNK - 1.0)) * (1.0 / ROW_CHUNK))
    nch_b = jnp.broadcast_to(nch, (N_EXPERTS, TB))
    choff = _dot(lstrict_ref[...], nch_b.astype(BF16))
    lo = choff * float(ROW_CHUNK)
    posm = jnp.where(sel > 0.0, lo + rank, -1.0)
    nch_ref[...] = nch_b[:, 0:128]

    sq_scr[0:N_EXPERTS, :] = lo[:, 0:128]
    sq_scr[N_EXPERTS:2 * N_EXPERTS, :] = (lo + nch_b * float(ROW_CHUNK))[:, 0:128]
    lh_ref[...] = sq_scr[...].T[0:8, :]

    pg_ref[0:N_EXPERTS, :] = posm
    pg_ref[N_EXPERTS:2 * N_EXPERTS, :] = posm
    pg_ref[2 * N_EXPERTS:3 * N_EXPERTS, :] = gates
    pg_ref[3 * N_EXPERTS:4 * N_EXPERTS, :] = gates

    _sorted_row_matrix(pg_ref, lh_ref, p_scr, weighted=False)
    h2b = h2.astype(BF16)
    for n in range(D_MODEL // 256):
        cols = slice(n * 256, (n + 1) * 256)
        xs_ref[:, cols] = _dot(p_scr[...], h2b[:, cols]).astype(BF16)


def _sorted_row_matrix(pg_ref, lh_ref, out_scr, weighted):
    posm2 = pg_ref[0:2 * N_EXPERTS, :]
    pos_hi = jnp.floor(posm2 * (1.0 / 256.0))
    pos_lo1 = posm2 - pos_hi * 256.0 + 1.0
    lh_row = lh_ref[0:1, :]
    sign = jnp.where(lax.broadcasted_iota(jnp.int32, (1, 2 * N_EXPERTS), 1) < N_EXPERTS, 1.0, -1.0)
    r_l = lax.broadcasted_iota(jnp.int32, (256, 2 * N_EXPERTS), 0).astype(F32)
    r_t1 = lax.broadcasted_iota(jnp.int32, (256, TB), 0).astype(F32) + 1.0
    if weighted:
        gates2 = pg_ref[2 * N_EXPERTS:4 * N_EXPERTS, :].astype(BF16)
    for grp in range(N_GRP):
        owner = jnp.where(r_l + float(grp * 256) >= lh_row, sign, 0.0).astype(BF16)
        low = jnp.where(pos_hi == float(grp), pos_lo1, 0.0).astype(BF16)
        hit = _dot(owner, low) == r_t1
        val = _dot(owner, gates2) if weighted else 1.0
        out_scr[grp * 256:(grp + 1) * 256, :] = jnp.where(hit, val, 0.0).astype(BF16)


def _router(x1, mod, rwt, rbias, ustrict, lstrict, blocks_per_batch, l):
    t, d = x1.shape
    nb = t // TB
    rb = RB_BLOCKS
    assert nb % rb == 0 and blocks_per_batch % rb == 0
    steps = nb // rb
    return pl.pallas_call(
        _router_kernel,
        grid=(steps + 1,),
        in_specs=[pl.BlockSpec((rb * TB, d), lambda i: (jnp.minimum(i, steps - 1), 0)),
                  pl.BlockSpec((None, None, 6, d),
                               lambda i: (l, jnp.minimum(i, steps - 1) // (blocks_per_batch // rb),
                                          0, 0)),
                  _layer(rwt, l), _layer(rbias, l), _full(ustrict.shape), _full(lstrict.shape)],
        out_specs=[pl.BlockSpec((rb * R_BLK, d), lambda i: (i, 0)),
                   pl.BlockSpec((rb, 4 * N_EXPERTS, TB), lambda i: (i, 0, 0)),
                   pl.BlockSpec((rb, N_EXPERTS, 128), lambda i: (i, 0, 0)),
                   pl.BlockSpec((rb, 8, 128), lambda i: (i, 0, 0))],
        out_shape=[jax.ShapeDtypeStruct(((nb + rb) * R_BLK, d), BF16),
                   jax.ShapeDtypeStruct((nb + rb, 4 * N_EXPERTS, TB), F32),
                   jax.ShapeDtypeStruct((nb + rb, N_EXPERTS, 128), F32),
                   jax.ShapeDtypeStruct((nb + rb, 8, 128), F32)],
        scratch_shapes=[pltpu.VMEM((rb, N_EXPERTS, TB), F32), pltpu.VMEM((rb, 128, 128), F32),
                        pltpu.VMEM((rb, R_BLK, TB), BF16)],
        compiler_params=pltpu.CompilerParams(
            dimension_semantics=("arbitrary",), vmem_limit_bytes=VMEM_LIMIT),
        name="router",
    )(x1, mod, rwt, rbias, ustrict, lstrict)


def _expert_kernel(te_ref, src_ref, dst_ref, nt_ref, xs_hbm, w1_ref, w3_ref, w2_ref, ys_hbm,
                   xbuf, ybuf, w1b, w3b, w2b, sem_in, sem_out):
    j = pl.program_id(0)
    nt = nt_ref[0]
    slot = lax.rem(j, NBUF)

    def in_copy(cid, sl, i):
        return pltpu.make_async_copy(xs_hbm.at[cid], xbuf.at[sl, i], sem_in.at[sl])

    def out_copy(cid, sl, i):
        return pltpu.make_async_copy(ybuf.at[sl, i], ys_hbm.at[cid], sem_out.at[sl])

    def start_in(tile, sl):
        for i in range(CPT):
            in_copy(src_ref[tile * CPT + i], sl, i).start()

    def wait_in(sl):
        for i in range(CPT):
            in_copy(0, sl, i).wait()

    def start_out(tile, sl):
        for i in range(CPT):
            out_copy(dst_ref[(tile + 1) * CPT + i], sl, i).start(priority=1)

    def wait_out(sl):
        for i in range(CPT):
            out_copy(0, sl, i).wait()

    @pl.when(j == 0)
    def _():
        ybuf[NBUF - 1] = jnp.zeros(ybuf.shape[1:], BF16)
        for t in range(NBUF - 1):
            start_in(t, t)

    @pl.when(j < nt)
    def _():
        wait_in(slot)
        prev = te_ref[jnp.maximum(j - 1, 0)]

        @pl.when((j == 0) | (te_ref[j] != prev))
        def _():
            w1b[...] = w1_ref[...].astype(BF16)
            w3b[...] = w3_ref[...].astype(BF16)
            w2b[...] = w2_ref[...].astype(BF16)

        @pl.when(j >= NBUF - 1)
        def _():
            wait_out(slot)

        xt = xbuf[slot].reshape(TILE_M, D_MODEL)
        gate = _dot(xt, w1b[...])
        up = _dot(xt, w3b[...])
        ahead = j + (NBUF - 1)
        start_in(ahead, lax.rem(ahead, NBUF))
        start_out(j - 1, lax.rem(ahead, NBUF))
        hid = (_silu(gate) * up).astype(BF16)
        ybuf[slot] = _dot(hid, w2b[...]).astype(BF16).reshape(CPT, ROW_CHUNK, D_MODEL)

        @pl.when(j == nt - 1)
        def _():
            start_out(j, slot)
            for back in range(NBUF):
                @pl.when(j + 1 >= back)
                def _():
                    wait_out(lax.rem(j + NBUF - back, NBUF))
            for fwd in range(1, NBUF):
                wait_in(lax.rem(j + fwd, NBUF))


def _experts(xs, te, src, dst, nt, w1, w3, w2, n_tiles_max, l):
    rows, d = xs.shape
    xs = xs.reshape(rows // ROW_CHUNK, ROW_CHUNK, d)
    grid_spec = pltpu.PrefetchScalarGridSpec(
        num_scalar_prefetch=4,
        grid=(n_tiles_max,),
        in_specs=[pl.BlockSpec(memory_space=pl.ANY),
                  pl.BlockSpec((None, None, d, D_EXPERT), lambda j, te, *_: (l, te[j], 0, 0)),
                  pl.BlockSpec((None, None, d, D_EXPERT), lambda j, te, *_: (l, te[j], 0, 0)),
                  pl.BlockSpec((None, None, D_EXPERT, d), lambda j, te, *_: (l, te[j], 0, 0))],
        out_specs=pl.BlockSpec(memory_space=pl.ANY),
        scratch_shapes=[pltpu.VMEM((NBUF, CPT, ROW_CHUNK, d), BF16),
                        pltpu.VMEM((NBUF, CPT, ROW_CHUNK, d), BF16),
                        pltpu.VMEM((d, D_EXPERT), BF16),
                        pltpu.VMEM((d, D_EXPERT), BF16),
                        pltpu.VMEM((D_EXPERT, d), BF16),
                        pltpu.SemaphoreType.DMA((NBUF,)),
                        pltpu.SemaphoreType.DMA((NBUF,))])
    return pl.pallas_call(
        _expert_kernel,
        grid_spec=grid_spec,
        out_shape=jax.ShapeDtypeStruct(xs.shape, xs.dtype),
        input_output_aliases={4: 0},
        compiler_params=pltpu.CompilerParams(
            dimension_semantics=("arbitrary",), vmem_limit_bytes=VMEM_LIMIT),
        name="experts",
    )(te, src, dst, nt, xs, w1, w3, w2).reshape(rows, d)


def _tile_plan(nch_f, n_tiles_max):
    nch = nch_f[:-RB_BLOCKS, :, 0]
    nb = nch.shape[0]
    hi = lax.Precision.HIGHEST
    choff = jnp.cumsum(nch, axis=1) - nch
    per_expert = jnp.sum(nch, axis=0)
    ntile = jnp.floor((per_expert + (CPT - 1.0)) * (1.0 / CPT))
    tile_end = jnp.cumsum(ntile)
    tile_start = tile_end - ntile
    nt = tile_end[-1]
    jt = jnp.arange(n_tiles_max, dtype=F32)
    te = jnp.minimum(jnp.sum((tile_end[None, :] <= jt[:, None]).astype(F32), axis=1), N_EXPERTS - 1.0)
    oh = (te[:, None] == jnp.arange(N_EXPERTS, dtype=F32)[None, :]).astype(F32)
    cum_t = jnp.dot(oh, jnp.cumsum(nch, axis=0).T, precision=hi)
    choff_t = jnp.dot(oh, choff.T, precision=hi)
    start_t = jnp.dot(oh, tile_start, precision=hi)
    total_t = jnp.dot(oh, per_expert, precision=hi)
    q = (jt - start_t)[:, None] * CPT + jnp.arange(CPT, dtype=F32)[None, :]
    le = cum_t[:, None, :] <= q[:, :, None]
    b = jnp.minimum(jnp.sum(le.astype(F32), axis=2), nb - 1.0)
    before = jnp.max(jnp.where(le, cum_t[:, None, :], 0.0), axis=2)
    is_b = jnp.arange(nb, dtype=F32)[None, None, :] == b[:, :, None]
    choff_s = jnp.sum(jnp.where(is_b, choff_t[:, None, :], 0.0), axis=2)
    valid = (jt[:, None] < nt) & (q < total_t[:, None])
    cid = b * (R_BLK // ROW_CHUNK) + choff_s + (q - before)
    assert (NBUF + 1) * CPT <= R_BLK // ROW_CHUNK
    lane = jnp.arange(CPT, dtype=F32)[None, :]
    spare0 = nb * (R_BLK // ROW_CHUNK)
    spare = spare0 + jnp.mod(jt, float(NBUF))[:, None] * CPT + lane
    zero_chunk = spare0 + NBUF * CPT + lane
    src = jnp.where(valid, cid, zero_chunk).astype(jnp.int32).reshape(-1)
    dst = jnp.concatenate([spare0 + (NBUF - 1) * CPT + lane, jnp.where(valid, cid, spare)],
                          axis=0).astype(jnp.int32).reshape(-1)
    return te.astype(jnp.int32), src, dst, nt.astype(jnp.int32).reshape(1)


def _combine_kernel(x_ref, mod_ref, pg_ref, lh_ref, ys_ref, sw1_ref, sw3_ref, sw2_ref, lng_ref,
                    lnb_ref, o_ref, w_scr, *, alpha):
    x = x_ref[...]
    sh2 = mod_ref[3:4, :]
    sc2 = mod_ref[4:5, :]
    g2 = mod_ref[5:6, :]
    h2b = (x * (1.0 + sc2) + sh2).astype(BF16)
    hid = (_silu(_dot(h2b, sw1_ref[...])) * _dot(h2b, sw3_ref[...])).astype(BF16)
    ffn = _dot(hid, sw2_ref[...])
    routed = []
    for c in range(RB_BLOCKS):
        _sorted_row_matrix(pg_ref.at[c], lh_ref.at[c], w_scr.at[c], weighted=True)
        routed.append(_dot_tn(ys_ref[c * R_BLK:(c + 1) * R_BLK, :], w_scr[c]).T)
    ffn = ffn + jnp.concatenate(routed, axis=0)
    o_ref[...] = _layer_norm(alpha * x + g2 * ffn, lng_ref[...], lnb_ref[...])


def _combine(x1, mod, pg, lh, ys, sw1, sw3, sw2, lng, lnb, blocks_per_batch, alpha, l):
    t, d = x1.shape
    nb = t // TB
    rb = RB_BLOCKS
    assert nb % rb == 0 and blocks_per_batch % rb == 0
    return pl.pallas_call(
        functools.partial(_combine_kernel, alpha=alpha),
        grid=(nb // rb,),
        in_specs=[pl.BlockSpec((rb * TB, d), lambda i: (i, 0)),
                  pl.BlockSpec((None, None, 6, d),
                               lambda i: (l, i // (blocks_per_batch // rb), 0, 0)),
                  pl.BlockSpec((rb, 4 * N_EXPERTS, TB), lambda i: (i, 0, 0)),
                  pl.BlockSpec((rb, 8, 128), lambda i: (i, 0, 0)),
                  pl.BlockSpec((rb * R_BLK, d), lambda i: (i, 0)),
                  _layer(sw1, l), _layer(sw3, l), _layer(sw2, l), _layer(lng, l), _layer(lnb, l)],
        out_specs=pl.BlockSpec((rb * TB, d), lambda i: (i, 0)),
        out_shape=jax.ShapeDtypeStruct((t, d), F32),
        scratch_shapes=[pltpu.VMEM((rb, R_BLK, TB), BF16)],
        compiler_params=pltpu.CompilerParams(
            dimension_semantics=("arbitrary",), vmem_limit_bytes=VMEM_LIMIT),
        name="combine",
    )(x1, mod, pg, lh, ys, sw1, sw3, sw2, lng, lnb)


def kernel(x, c, w_ada, b_ada, w_in, lru_conv_w, lru_conv_b, lru_w_a, lru_b_a, lru_w_x, lru_b_x, lru_lambda, gla_w_alpha, gla_b_alpha, gla_norm, ssd_conv_w, ssd_conv_b, ssd_dt_bias, ssd_a_log, ssd_d, ssd_norm, w_out, ln1_g, ln1_b, router_w, router_bias, exp_w1, exp_w3, exp_w2, shared_w1, shared_w3, shared_w2, ln2_g, ln2_b):
    bsz, seq, d = x.shape
    n_layers = w_ada.shape[0]
    assert d == D_MODEL and seq % TS == 0 and seq % TB == 0
    alpha = float((2 * n_layers) ** 0.25)
    t = bsz * seq
    nb = t // TB
    blocks_per_batch = seq // TB
    n_tiles_max = -(-(nb * (R_BLK // ROW_CHUNK)) // CPT) + N_EXPERTS

    mod = _adaln(c, w_ada, b_ada).reshape(n_layers, bsz, 6, d)

    win = _relayout_w_in(w_in)
    eye = jnp.eye(LRU_BLOCKS, dtype=F32)
    bd = lambda w: jnp.einsum("lnij,nm->lnimj", w, eye).reshape(n_layers, LRU_WIDTH, LRU_WIDTH)
    wg = jnp.concatenate([bd(lru_w_a), bd(lru_w_x)], axis=-1).astype(BF16)
    bg = jnp.concatenate([lru_b_a, lru_b_x], axis=-1)[:, None, :]
    wal = jnp.concatenate([gla_w_alpha, jnp.zeros((n_layers, 128 - GLA_RANK, GLA_QK), F32)],
                          axis=1).astype(BF16)
    pad_dt = lambda v: jnp.pad(v, ((0, 0), (RD_DT0, 128 - RD_DT0 - SSD_HEADS)))[:, None, :]
    e_np = np.zeros((128, SSD_WIDTH), np.float32)
    for hh in range(SSD_HEADS):
        e_np[RD_DT0 + hh, hh * SSD_HEADDIM:(hh + 1) * SSD_HEADDIM] = 1.0
    e_mat = jnp.asarray(e_np, BF16)
    shift_mat = jnp.asarray(
        np.concatenate([np.eye(TS, k=-(CONV_W - 1 - k), dtype=np.float32)
                        for k in range(CONV_W - 1)], axis=0), BF16)
    wout = w_out.astype(BF16)
    rwt = jnp.swapaxes(router_w, 1, 2)
    ustrict = jnp.asarray(np.triu(np.ones((TB, TB), np.float32), 1), BF16)
    lstrict = jnp.asarray(np.tril(np.ones((N_EXPERTS, N_EXPERTS), np.float32), -1), BF16)
    sw1 = shared_w1.astype(BF16)
    sw3 = shared_w3.astype(BF16)
    sw2 = shared_w2.astype(BF16)

    row = lambda v: v[:, None, :]
    p = dict(win=win, lcw=lru_conv_w, lcb=row(lru_conv_b), wg=wg, bg=bg, lam=row(lru_lambda),
             wal=wal, bal=row(gla_b_alpha), gnorm=row(jnp.tile(gla_norm, (1, GLA_HEADS))),
             scw=ssd_conv_w, scb=row(ssd_conv_b), dtb=pad_dt(ssd_dt_bias), alog=pad_dt(ssd_a_log),
             dskip=row(jnp.repeat(ssd_d, SSD_HEADDIM, axis=1)), snorm=row(ssd_norm),
             e=e_mat, shift=shift_mat, wout=wout, lng=row(ln1_g), lnb=row(ln1_b))
    rbias = router_bias[:, :, None]
    ln2g, ln2b = row(ln2_g), row(ln2_b)

    for l in range(n_layers):
        x1 = _mixer(x, mod, p, alpha, l).reshape(t, d)
        xs, pg, nch, lh = _router(x1, mod, rwt, rbias, ustrict, lstrict, blocks_per_batch, l)
        te, src, dst, nt = _tile_plan(nch, n_tiles_max + NBUF - 1)
        ys = _experts(xs, te, src, dst, nt, exp_w1, exp_w3, exp_w2, n_tiles_max, l)
        x = _combine(x1, mod, pg, lh, ys, sw1, sw3, sw2, ln2g, ln2b, blocks_per_batch, alpha,
                     l).reshape(bsz, seq, d)
    return x
```

```python
import functools

import jax
import jax.numpy as jnp
import numpy as np
from jax import lax
from jax.experimental import pallas as pl
from jax.experimental.pallas import tpu as pltpu

F32 = jnp.float32
BF16 = jnp.bfloat16

D_MODEL = 1024
CHUNK = 64
LRU_WIDTH = 512
LRU_BLOCKS = 8
LRU_C = 8.0
CONV_W = 4
GLA_WIDTH = 512
GLA_HEADS = 4
GLA_DV = 128
GLA_DK = 64
GLA_QK = 256
GLA_RANK = 16
GLA_TAU = 16.0
SSD_WIDTH = 1024
SSD_HEADDIM = 64
SSD_HEADS = 16
SSD_GROUPS = 2
SSD_HPG = 8
SSD_STATE = 128
SSD_CONV_DIM = SSD_WIDTH + 2 * SSD_GROUPS * SSD_STATE
D_MIX = 2048
N_EXPERTS = 64
TOP_K = 8
N_EXPERT_GROUPS = 8
EXPERTS_PER_GROUP = 8
TOPK_GROUPS = 4
D_EXPERT = 256
ROUTE_SCALE = 2.5
LN_EPS = 1e-5
RMS_EPS = 1e-6

C_LRU_X = 0
C_LRU_G = 512
C_Q = 1024
C_K = 1280
C_V = 1536
C_G = 2048
C_Z = 2560
C_XBC = 3584
C_RD = 5120
D_IN_PAD = 5248
RD_DT0 = 16

TS = 256
MIX_ROWS = 2
N_MIX_WEIGHTS = 20
TB = 256
RB_BLOCKS = 2
ROW_CHUNK = 16
CPT = 32
TILE_M = ROW_CHUNK * CPT
NBUF = 4
R_BLK = -(-(TOP_K * TB + N_EXPERTS * (ROW_CHUNK - 1)) // 256) * 256
N_GRP = R_BLK // 256
VMEM_LIMIT = 56 * 1024 * 1024


def _dot(a, b):
    return jnp.dot(a, b, preferred_element_type=F32)


def _dot_nt(a, b):
    return lax.dot_general(a, b, (((1,), (1,)), ((), ())), preferred_element_type=F32)


def _dot_tn(a, b):
    return lax.dot_general(a, b, (((0,), (0,)), ((), ())), preferred_element_type=F32)


def _sigmoid(x):
    return jax.nn.sigmoid(x)


def _silu(x):
    return x * _sigmoid(x)


def _softplus(x):
    return jnp.maximum(x, 0.0) + jnp.log1p(jnp.exp(-jnp.abs(x)))


def _gelu_tanh(x):
    return 0.5 * x * (1.0 + jnp.tanh(0.7978845608028654 * (x + 0.044715 * (x * x * x))))


def _layer_norm(x, g, b):
    mu = jnp.mean(x, axis=-1, keepdims=True)
    xc = x - mu
    var = jnp.mean(xc * xc, axis=-1, keepdims=True)
    return xc * lax.rsqrt(var + LN_EPS) * g + b


def _shift_rows(x, d, fill, rowmod):
    return jnp.where(rowmod >= d, pltpu.roll(x, d, axis=0), fill)


def _cumsum_rows(x, period):
    rowmod = lax.broadcasted_iota(jnp.int32, x.shape, 0) & (period - 1)
    d = 1
    while d < period:
        x = x + _shift_rows(x, d, 0.0, rowmod)
        d *= 2
    return x


def _causal_conv(carry_ref, x, w_ref, b_ref, shift_ref):
    n = x.shape[0]
    delayed = _dot(shift_ref[...], x.astype(BF16))
    y = b_ref[...] + w_ref[CONV_W - 1:CONV_W, :] * x
    carry = carry_ref[...]
    row8 = lax.broadcasted_iota(jnp.int32, (8, 1), 0)
    head = jnp.zeros((8, x.shape[1]), F32)
    for k in range(CONV_W - 1):
        delay = CONV_W - 1 - k
        y = y + w_ref[k:k + 1, :] * delayed[k * n:(k + 1) * n, :]
        head = head + w_ref[k:k + 1, :] * jnp.where(row8 < delay, pltpu.roll(carry, delay, axis=0), 0.0)
    carry_ref[...] = x[n - 8:n, :]
    return jnp.concatenate([y[0:8, :] + head, y[8:n, :]], axis=0)


def _adaln_kernel(c_ref, w_ref, b_ref, o_ref):
    cond = _silu(c_ref[...])
    o_ref[...] = jnp.dot(cond, w_ref[...], preferred_element_type=F32,
                         precision=lax.Precision.HIGHEST) + b_ref[...]


def _adaln(c, w_ada, b_ada):
    n_layers, d, n = w_ada.shape
    bsz = c.shape[0]
    tn = 1024
    return pl.pallas_call(
        _adaln_kernel,
        grid=(n_layers, n // tn),
        in_specs=[pl.BlockSpec((bsz, d), lambda l, j: (0, 0)),
                  pl.BlockSpec((None, d, tn), lambda l, j: (l, 0, j)),
                  pl.BlockSpec((None, 1, tn), lambda l, j: (l, 0, j))],
        out_specs=pl.BlockSpec((None, bsz, tn), lambda l, j: (l, 0, j)),
        out_shape=jax.ShapeDtypeStruct((n_layers, bsz, n), F32),
        name="adaln",
    )(c, w_ada, b_ada.reshape(n_layers, 1, n))


O_R = 2 * LRU_WIDTH + 2 * GLA_QK + 2 * GLA_WIDTH
O_Z = O_R + GLA_RANK
O_DT = O_Z + SSD_WIDTH + SSD_CONV_DIM
D_IN_PROJ = O_DT + SSD_HEADS


def _relayout_kernel(w_ref, o_ref):
    o_ref[:, 0:O_R] = w_ref[:, 0:O_R].astype(BF16)
    o_ref[:, C_Z:C_RD] = w_ref[:, O_Z:O_DT].astype(BF16)
    tail = jnp.concatenate(
        [w_ref[:, O_R:O_Z], w_ref[:, O_DT:D_IN_PROJ],
         jnp.zeros((w_ref.shape[0], 128 - GLA_RANK - SSD_HEADS), F32)], axis=1)
    o_ref[:, C_RD:D_IN_PAD] = tail.astype(BF16)


def _relayout_w_in(w_in):
    n_layers, d, n = w_in.shape
    assert n == D_IN_PROJ and O_R == C_Z and C_RD - C_Z == O_DT - O_Z
    rows = 256
    return pl.pallas_call(
        _relayout_kernel,
        grid=(n_layers, d // rows),
        in_specs=[pl.BlockSpec((None, rows, n), lambda l, i: (l, i, 0))],
        out_specs=pl.BlockSpec((None, rows, D_IN_PAD), lambda l, i: (l, i, 0)),
        out_shape=jax.ShapeDtypeStruct((n_layers, d, D_IN_PAD), BF16),
        name="relayout_w_in",
    )(w_in)


def _mixer_kernel(x_ref, mod_ref, *refs, alpha):
    weights, o_ref, scratch = refs[:N_MIX_WEIGHTS], refs[N_MIX_WEIGHTS], refs[N_MIX_WEIGHTS + 1:]

    @pl.when(pl.program_id(1) == 0)
    def _():
        for s_ref in scratch[:5]:
            s_ref[...] = jnp.zeros(s_ref.shape, F32)

    _mixer_tile(x_ref, mod_ref, *weights, o_ref, *scratch, alpha=alpha)


def _mixer_tile(x_ref, mod_ref, win_ref, lcw_ref, lcb_ref, wg_ref, bg_ref, lam_ref,
                wal_ref, bal_ref, gnorm_ref, scw_ref, scb_ref, dtb_ref, alog_ref,
                dskip_ref, snorm_ref, e_ref, shift_ref, wout_ref, lng_ref, lnb_ref,
                o_ref,
                stage_l, stage_s, hcar, gla_st, ssd_st, o_scr, cat, *, alpha):
    chains = range(MIX_ROWS)
    rows_of = lambda c: slice(c * TS, (c + 1) * TS)
    stack = lambda parts: jnp.concatenate(parts, axis=0)
    per_chain = lambda fn: stack([jnp.broadcast_to(fn(c), (TS, fn(c).shape[1])) for c in chains])

    x = x_ref[...].reshape(MIX_ROWS * TS, D_MODEL)
    sh1 = per_chain(lambda c: mod_ref[c, 0:1, :])
    sc1 = per_chain(lambda c: mod_ref[c, 1:2, :])
    g1 = per_chain(lambda c: mod_ref[c, 2:3, :])
    h = (x * (1.0 + sc1) + sh1).astype(BF16)
    row = lax.broadcasted_iota(jnp.int32, (MIX_ROWS * TS, 1), 0) & (TS - 1)

    lx = _dot(h, win_ref[:, C_LRU_X:C_LRU_X + LRU_WIDTH])
    xc = stack([_causal_conv(stage_l.at[c], lx[rows_of(c)], lcw_ref, lcb_ref, shift_ref)
                for c in chains])
    gts = _dot(xc.astype(BF16), wg_ref[...]) + bg_ref[...]
    r = _sigmoid(gts[:, :LRU_WIDTH])
    i_g = _sigmoid(gts[:, LRU_WIDTH:])
    log_a = (-LRU_C) * r * _softplus(-lam_ref[...])
    a = jnp.exp(log_a)
    u = jnp.sqrt(-jnp.tanh(log_a) * (a * a + 1.0)) * (i_g * xc)
    row8 = row & 7
    acc_a, acc_h = a, u
    d = 1
    while d < 8:
        sh_h = jnp.where(row8 >= d, pltpu.roll(acc_h, d, axis=0), 0.0)
        sh_a = jnp.where(row8 >= d, pltpu.roll(acc_a, d, axis=0), 1.0)
        acc_h = acc_h + acc_a * sh_h
        acc_a = acc_a * sh_a
        d *= 2
    groups = []
    for c in chains:
        carry = hcar[c, 0:1, :]
        for g in range(TS // 8):
            r0 = c * TS + 8 * g
            hg = acc_h[r0:r0 + 8, :] + acc_a[r0:r0 + 8, :] * carry
            carry = hg[7:8, :]
            groups.append(hg)
        hcar[c, 0:1, :] = carry
    h_lru = stack(groups)
    lgate = _dot(h, win_ref[:, C_LRU_G:C_LRU_G + LRU_WIDTH])
    cat[:, 0:LRU_WIDTH] = (h_lru * _gelu_tanh(lgate)).astype(BF16)

    rd = _dot(h, win_ref[:, C_RD:C_RD + 128])

    q = _dot(h, win_ref[:, C_Q:C_Q + GLA_QK]) * (GLA_DK ** -0.5)
    k = _dot(h, win_ref[:, C_K:C_K + GLA_QK])
    v = _dot(h, win_ref[:, C_V:C_V + GLA_WIDTH]).astype(BF16)
    zal = _dot(rd.astype(BF16), wal_ref[...]) + bal_ref[...]
    log_alpha = (jnp.minimum(zal, 0.0) - jnp.log1p(jnp.exp(-jnp.abs(zal)))) * (1.0 / GLA_TAU)
    cum = _cumsum_rows(log_alpha, CHUNK)
    q_dec = q * jnp.exp(cum)
    k_dec = (k * jnp.exp(-cum)).astype(BF16)
    ri = lax.broadcasted_iota(jnp.int32, (TS, TS), 0)
    ci = lax.broadcasted_iota(jnp.int32, (TS, TS), 1)
    causal = ri >= ci
    blk_causal = causal & ((ri // CHUNK) == (ci // CHUNK))
    lane_head = lax.broadcasted_iota(jnp.int32, (1, GLA_QK), 1) // GLA_DK
    for hd in range(GLA_HEADS):
        qm = jnp.where(lane_head == hd, q_dec, 0.0).astype(BF16)
        for c in chains:
            att = jnp.where(blk_causal, _dot_nt(qm[rows_of(c)], k_dec[rows_of(c)]), 0.0)
            o_scr[rows_of(c), hd * GLA_DV:(hd + 1) * GLA_DV] = _dot(
                att.astype(BF16), v[rows_of(c), hd * GLA_DV:(hd + 1) * GLA_DV])
    sr = lax.broadcasted_iota(jnp.int32, (GLA_WIDTH, GLA_QK), 0) // GLA_DV
    scol = lax.broadcasted_iota(jnp.int32, (GLA_WIDTH, GLA_QK), 1) // GLA_DK
    st_mask = sr == scol
    q_dec_b = q_dec.astype(BF16)
    for c in chains:
        for ch in range(TS // CHUNK):
            rows = slice(c * TS + ch * CHUNK, c * TS + (ch + 1) * CHUNK)
            st = gla_st[c]
            o_scr[rows, :] = o_scr[rows, :] + _dot_nt(q_dec_b[rows, :], st.astype(BF16))
            cum_c = cum[rows, :]
            cum_last = cum_c[CHUNK - 1:CHUNK, :]
            k_end = (k[rows, :] * jnp.exp(cum_last - cum_c)).astype(BF16)
            upd = _dot_tn(v[rows, :], k_end)
            gla_st[c] = st * jnp.exp(cum_last) + jnp.where(st_mask, upd, 0.0)
    g_gate = _dot(h, win_ref[:, C_G:C_G + GLA_WIDTH])
    for hd in range(GLA_HEADS):
        cols = slice(hd * GLA_DV, (hd + 1) * GLA_DV)
        oh = o_scr[:, cols]
        ms = jnp.mean(oh * oh, axis=-1, keepdims=True)
        on = oh * lax.rsqrt(ms + RMS_EPS) * gnorm_ref[:, cols]
        cat[:, LRU_WIDTH + hd * GLA_DV:LRU_WIDTH + (hd + 1) * GLA_DV] = (
            on * _silu(g_gate[:, cols])).astype(BF16)

    xbc = _dot(h, win_ref[:, C_XBC:C_XBC + SSD_CONV_DIM])
    xbc = _silu(stack([_causal_conv(stage_s.at[c], xbc[rows_of(c)], scw_ref, scb_ref, shift_ref)
                       for c in chains]))
    xs = xbc[:, :SSD_WIDTH]
    lane128 = lax.broadcasted_iota(jnp.int32, (1, 128), 1)
    dt_lane = (lane128 >= RD_DT0) & (lane128 < RD_DT0 + SSD_HEADS)
    a_neg = jnp.where(dt_lane, -jnp.exp(alog_ref[...]), 0.0)
    dtp = _softplus(rd + dtb_ref[...])
    a_cs = _cumsum_rows(dtp * a_neg, TS)
    a_last = per_chain(lambda c: a_cs[(c + 1) * TS - 1:(c + 1) * TS, :])
    ea = jnp.exp(a_cs)
    dte = jnp.exp(a_last - a_cs)
    n_rows = MIX_ROWS * TS
    expand = _dot(jnp.concatenate([dtp, dte, ea], axis=0).astype(BF16), e_ref[...])
    dt_x = expand[0:n_rows, :]
    dte_x = expand[n_rows:2 * n_rows, :]
    ea_x = expand[2 * n_rows:3 * n_rows, :]
    xdt = xs * dt_x
    xdt_b = xdt.astype(BF16)
    xdte_b = (xdt * dte_x).astype(BF16)
    a_cs_t = a_cs.T
    z = _dot(h, win_ref[:, C_Z:C_Z + SSD_WIDTH])
    lane_lo = lax.broadcasted_iota(jnp.int32, (1, 128), 1) < SSD_HEADDIM
    gw = SSD_HPG * SSD_HEADDIM
    for g in range(SSD_GROUPS):
        gcols = slice(g * gw, (g + 1) * gw)
        bm = xbc[:, SSD_WIDTH + g * SSD_STATE:SSD_WIDTH + (g + 1) * SSD_STATE].astype(BF16)
        cm = xbc[:, SSD_WIDTH + (SSD_GROUPS + g) * SSD_STATE:
                 SSD_WIDTH + (SSD_GROUPS + g + 1) * SSD_STATE].astype(BF16)
        y_chain = []
        for c in chains:
            rc = rows_of(c)
            cb = _dot_nt(cm[rc], bm[rc])
            st = ssd_st[c, g]
            y_off = _dot(cm[rc], st.astype(BF16)) * ea_x[rc, gcols]
            y_pairs = []
            for j in range(SSD_HPG // 2):
                ys = []
                for hh in (2 * j, 2 * j + 1):
                    lane = RD_DT0 + g * SSD_HPG + hh
                    diff = a_cs[rc, lane:lane + 1] - a_cs_t[lane:lane + 1, rc]
                    m = (cb * jnp.exp(jnp.where(causal, diff, -jnp.inf))).astype(BF16)
                    ys.append(_dot(m, xdt_b[rc, g * gw + j * 128:g * gw + (j + 1) * 128]))
                y_pairs.append(jnp.where(lane_lo, ys[0], ys[1]) + y_off[:, j * 128:(j + 1) * 128])
            y_chain.append(y_pairs)
            ssd_st[c, g] = (st * ea_x[(c + 1) * TS - 1:(c + 1) * TS, gcols]
                            + _dot_tn(bm[rc], xdte_b[rc, gcols]))
        yg = []
        for j in range(SSD_HPG // 2):
            cols = slice(g * gw + j * 128, g * gw + (j + 1) * 128)
            y = stack([y_chain[c][j] for c in chains]) + xs[:, cols] * dskip_ref[:, cols]
            yg.append(y * _silu(z[:, cols]))
        ssq = yg[0] * yg[0]
        for t in yg[1:]:
            ssq = ssq + t * t
        ms = jnp.sum(ssq, axis=-1, keepdims=True) * (1.0 / gw)
        inv = lax.rsqrt(ms + RMS_EPS)
        for j, t in enumerate(yg):
            cols = slice(g * gw + j * 128, g * gw + (j + 1) * 128)
            cat[:, LRU_WIDTH + GLA_WIDTH + g * gw + j * 128:
                LRU_WIDTH + GLA_WIDTH + g * gw + (j + 1) * 128] = (
                    t * inv * snorm_ref[:, cols]).astype(BF16)

    mix = _dot(cat[...], wout_ref[...])
    out = _layer_norm(alpha * x + g1 * mix, lng_ref[...], lnb_ref[...])
    o_ref[...] = out.reshape(MIX_ROWS, TS, D_MODEL)


def _full(shape):
    nd = len(shape)
    return pl.BlockSpec(shape, lambda *_: (0,) * nd)


def _layer(arr, l):
    nd = arr.ndim
    return pl.BlockSpec((None,) + arr.shape[1:], lambda *_: (l,) + (0,) * (nd - 1))


def _mixer(x, mod, p, alpha, l):
    bsz, seq, d = x.shape
    names = ["win", "lcw", "lcb", "wg", "bg", "lam", "wal", "bal", "gnorm", "scw", "scb", "dtb",
             "alog", "dskip", "snorm", "e", "shift", "wout", "lng", "lnb"]
    weights = [p[n] for n in names]
    assert bsz % MIX_ROWS == 0 and len(names) == N_MIX_WEIGHTS
    in_specs = [pl.BlockSpec((MIX_ROWS, TS, d), lambda b, s: (b, s, 0)),
                pl.BlockSpec((None, MIX_ROWS, 6, d), lambda b, s: (l, b, 0, 0))]
    in_specs += [_full(w.shape) if n in ("e", "shift") else _layer(w, l)
                 for n, w in zip(names, weights)]
    return pl.pallas_call(
        functools.partial(_mixer_kernel, alpha=alpha),
        grid=(bsz // MIX_ROWS, seq // TS),
        in_specs=in_specs,
        out_specs=pl.BlockSpec((MIX_ROWS, TS, d), lambda b, s: (b, s, 0)),
        out_shape=jax.ShapeDtypeStruct((bsz, seq, d), F32),
        scratch_shapes=[pltpu.VMEM((MIX_ROWS, 8, LRU_WIDTH), F32),
                        pltpu.VMEM((MIX_ROWS, 8, SSD_CONV_DIM), F32),
                        pltpu.VMEM((MIX_ROWS, 8, LRU_WIDTH), F32),
                        pltpu.VMEM((MIX_ROWS, GLA_WIDTH, GLA_QK), F32),
                        pltpu.VMEM((MIX_ROWS, SSD_GROUPS, SSD_STATE, SSD_HPG * SSD_HEADDIM), F32),
                        pltpu.VMEM((MIX_ROWS * TS, GLA_WIDTH), F32),
                        pltpu.VMEM((MIX_ROWS * TS, D_MIX), BF16)],
        compiler_params=pltpu.CompilerParams(
            dimension_semantics=("arbitrary", "arbitrary"), vmem_limit_bytes=VMEM_LIMIT),
        name="mixer",
    )(x, mod, *weights)


def _router_kernel(*refs):
    last = pl.program_id(0) == pl.num_programs(0) - 1

    @pl.when(jnp.logical_not(last))
    def _():
        x_ref, mod_ref, rwt_ref, rbias_ref, ustrict_ref, lstrict_ref = refs[:6]
        xs_ref, pg_ref, nch_ref, lh_ref, msk_scr, sq_scr, p_scr = refs[6:]
        for c in range(RB_BLOCKS):
            _route_block(x_ref.at[pl.ds(c * TB, TB)], mod_ref, rwt_ref, rbias_ref, ustrict_ref,
                         lstrict_ref, xs_ref.at[pl.ds(c * R_BLK, R_BLK)], pg_ref.at[c],
                         nch_ref.at[c], lh_ref.at[c], msk_scr.at[c], sq_scr.at[c], p_scr.at[c])

    @pl.when(last)
    def _():
        for o_ref in refs[6:10]:
            o_ref[...] = jnp.zeros(o_ref.shape, o_ref.dtype)


def _route_block(x_ref, mod_ref, rwt_ref, rbias_ref, ustrict_ref, lstrict_ref,
                 xs_ref, pg_ref, nch_ref, lh_ref, msk_scr, sq_scr, p_scr):
    x = x_ref[...]
    sh2 = mod_ref[3:4, :]
    sc2 = mod_ref[4:5, :]
    h2 = x * (1.0 + sc2) + sh2
    logits = lax.dot_general(rwt_ref[...], h2, (((1,), (1,)), ((), ())),
                             preferred_element_type=F32, precision=lax.Precision.HIGHEST)
    scores = _sigmoid(logits)
    biased = scores + rbias_ref[...]

    row8 = lax.broadcasted_iota(jnp.int32, (EXPERTS_PER_GROUP, TB), 0)
    gsc = []
    for g in range(N_EXPERT_GROUPS):
        vals = biased[g * EXPERTS_PER_GROUP:(g + 1) * EXPERTS_PER_GROUP, :]
        m1 = jnp.max(vals, axis=0, keepdims=True)
        i1 = jnp.min(jnp.where(vals == m1, row8, EXPERTS_PER_GROUP), axis=0, keepdims=True)
        m2 = jnp.max(jnp.where(row8 == i1, -jnp.inf, vals), axis=0, keepdims=True)
        gsc.append(m1 + m2)
    for g in range(N_EXPERT_GROUPS):
        beaten = jnp.zeros((1, TB), F32)
        for o in range(N_EXPERT_GROUPS):
            if o == g:
                continue
            wins = (gsc[o] >= gsc[g]) if o < g else (gsc[o] > gsc[g])
            beaten = beaten + jnp.where(wins, 1.0, 0.0)
        keep = beaten < TOPK_GROUPS
        vals = biased[g * EXPERTS_PER_GROUP:(g + 1) * EXPERTS_PER_GROUP, :]
        msk_scr[g * EXPERTS_PER_GROUP:(g + 1) * EXPERTS_PER_GROUP, :] = jnp.where(keep, vals, -jnp.inf)

    cur = msk_scr[...]
    row64 = lax.broadcasted_iota(jnp.int32, (N_EXPERTS, TB), 0)
    sel = jnp.zeros((N_EXPERTS, TB), F32)
    for _ in range(TOP_K):
        m = jnp.max(cur, axis=0, keepdims=True)
        ik = jnp.min(jnp.where(cur == m, row64, N_EXPERTS), axis=0, keepdims=True)
        hit = row64 == ik
        sel = jnp.where(hit, 1.0, sel)
        cur = jnp.where(hit, -jnp.inf, cur)
    wsel = sel * scores
    gates = wsel * (ROUTE_SCALE / jnp.sum(wsel, axis=0, keepdims=True))

    rank = _dot(sel.astype(BF16), ustrict_ref[...])
    cnt = jnp.sum(sel, axis=1, keepdims=True)
    nch = jnp.floor((cnt + (ROW_CHUNK - 1.0)) * (1.0 / ROW_CHUNK))
    nch_b = jnp.broadcast_to(nch, (N_EXPERTS, TB))
    choff = _dot(lstrict_ref[...], nch_b.astype(BF16))
    lo = choff * float(ROW_CHUNK)
    posm = jnp.where(sel > 0.0, lo + rank, -1.0)
    nch_ref[...] = nch_b[:, 0:128]

    sq_scr[0:N_EXPERTS, :] = lo[:, 0:128]
    sq_scr[N_EXPERTS:2 * N_EXPERTS, :] = (lo + nch_b * float(ROW_CHUNK))[:, 0:128]
    lh_ref[...] = sq_scr[...].T[0:8, :]

    pg_ref[0:N_EXPERTS, :] = posm
    pg_ref[N_EXPERTS:2 * N_EXPERTS, :] = posm
    pg_ref[2 * N_EXPERTS:3 * N_EXPERTS, :] = gates
    pg_ref[3 * N_EXPERTS:4 * N_EXPERTS, :] = gates

    _sorted_row_matrix(pg_ref, lh_ref, p_scr, weighted=False)
    h2b = h2.astype(BF16)
    for n in range(D_MODEL // 256):
        cols = slice(n * 256, (n + 1) * 256)
        xs_ref[:, cols] = _dot(p_scr[...], h2b[:, cols]).astype(BF16)


def _sorted_row_matrix(pg_ref, lh_ref, out_scr, weighted):
    posm2 = pg_ref[0:2 * N_EXPERTS, :]
    pos_hi = jnp.floor(posm2 * (1.0 / 256.0))
    pos_lo1 = posm2 - pos_hi * 256.0 + 1.0
    lh_row = lh_ref[0:1, :]
    sign = jnp.where(lax.broadcasted_iota(jnp.int32, (1, 2 * N_EXPERTS), 1) < N_EXPERTS, 1.0, -1.0)
    r_l = lax.broadcasted_iota(jnp.int32, (256, 2 * N_EXPERTS), 0).astype(F32)
    r_t1 = lax.broadcasted_iota(jnp.int32, (256, TB), 0).astype(F32) + 1.0
    if weighted:
        gates2 = pg_ref[2 * N_EXPERTS:4 * N_EXPERTS, :].astype(BF16)
    for grp in range(N_GRP):
        owner = jnp.where(r_l + float(grp * 256) >= lh_row, sign, 0.0).astype(BF16)
        low = jnp.where(pos_hi == float(grp), pos_lo1, 0.0).astype(BF16)
        hit = _dot(owner, low) == r_t1
        val = _dot(owner, gates2) if weighted else 1.0
        out_scr[grp * 256:(grp + 1) * 256, :] = jnp.where(hit, val, 0.0).astype(BF16)


def _router(x1, mod, rwt, rbias, ustrict, lstrict, blocks_per_batch, l):
    t, d = x1.shape
    nb = t // TB
    rb = RB_BLOCKS
    assert nb % rb == 0 and blocks_per_batch % rb == 0
    steps = nb // rb
    return pl.pallas_call(
        _router_kernel,
        grid=(steps + 1,),
        in_specs=[pl.BlockSpec((rb * TB, d), lambda i: (jnp.minimum(i, steps - 1), 0)),
                  pl.BlockSpec((None, None, 6, d),
                               lambda i: (l, jnp.minimum(i, steps - 1) // (blocks_per_batch // rb),
                                          0, 0)),
                  _layer(rwt, l), _layer(rbias, l), _full(ustrict.shape), _full(lstrict.shape)],
        out_specs=[pl.BlockSpec((rb * R_BLK, d), lambda i: (i, 0)),
                   pl.BlockSpec((rb, 4 * N_EXPERTS, TB), lambda i: (i, 0, 0)),
                   pl.BlockSpec((rb, N_EXPERTS, 128), lambda i: (i, 0, 0)),
                   pl.BlockSpec((rb, 8, 128), lambda i: (i, 0, 0))],
        out_shape=[jax.ShapeDtypeStruct(((nb + rb) * R_BLK, d), BF16),
                   jax.ShapeDtypeStruct((nb + rb, 4 * N_EXPERTS, TB), F32),
                   jax.ShapeDtypeStruct((nb + rb, N_EXPERTS, 128), F32),
                   jax.ShapeDtypeStruct((nb + rb, 8, 128), F32)],
        scratch_shapes=[pltpu.VMEM((rb, N_EXPERTS, TB), F32), pltpu.VMEM((rb, 128, 128), F32),
                        pltpu.VMEM((rb, R_BLK, TB), BF16)],
        compiler_params=pltpu.CompilerParams(
            dimension_semantics=("arbitrary",), vmem_limit_bytes=VMEM_LIMIT),
        name="router",
    )(x1, mod, rwt, rbias, ustrict, lstrict)


def _expert_kernel(te_ref, src_ref, dst_ref, nt_ref, xs_hbm, w1_ref, w3_ref, w2_ref, ys_hbm,
                   xbuf, ybuf, w1b, w3b, w2b, sem_in, sem_out):
    j = pl.program_id(0)
    nt = nt_ref[0]
    slot = lax.rem(j, NBUF)

    def in_copy(cid, sl, i):
        return pltpu.make_async_copy(xs_hbm.at[cid], xbuf.at[sl, i], sem_in.at[sl])

    def out_copy(cid, sl, i):
        return pltpu.make_async_copy(ybuf.at[sl, i], ys_hbm.at[cid], sem_out.at[sl])

    def start_in(tile, sl):
        for i in range(CPT):
            in_copy(src_ref[tile * CPT + i], sl, i).start()

    def wait_in(sl):
        for i in range(CPT):
            in_copy(0, sl, i).wait()

    def start_out(tile, sl):
        for i in range(CPT):
            out_copy(dst_ref[(tile + 1) * CPT + i], sl, i).start(priority=1)

    def wait_out(sl):
        for i in range(CPT):
            out_copy(0, sl, i).wait()

    @pl.when(j == 0)
    def _():
        ybuf[NBUF - 1] = jnp.zeros(ybuf.shape[1:], BF16)
        for t in range(NBUF - 1):
            start_in(t, t)

    @pl.when(j < nt)
    def _():
        wait_in(slot)
        prev = te_ref[jnp.maximum(j - 1, 0)]

        @pl.when((j == 0) | (te_ref[j] != prev))
        def _():
            w1b[...] = w1_ref[...].astype(BF16)
            w3b[...] = w3_ref[...].astype(BF16)
            w2b[...] = w2_ref[...].astype(BF16)

        @pl.when(j >= NBUF - 1)
        def _():
            wait_out(slot)

        xt = xbuf[slot].reshape(TILE_M, D_MODEL)
        gate = _dot(xt, w1b[...])
        up = _dot(xt, w3b[...])
        ahead = j + (NBUF - 1)
        start_in(ahead, lax.rem(ahead, NBUF))
        start_out(j - 1, lax.rem(ahead, NBUF))
        hid = (_silu(gate) * up).astype(BF16)
        ybuf[slot] = _dot(hid, w2b[...]).astype(BF16).reshape(CPT, ROW_CHUNK, D_MODEL)

        @pl.when(j == nt - 1)
        def _():
            start_out(j, slot)
            for back in range(NBUF):
                @pl.when(j + 1 >= back)
                def _():
                    wait_out(lax.rem(j + NBUF - back, NBUF))
            for fwd in range(1, NBUF):
                wait_in(lax.rem(j + fwd, NBUF))


def _experts(xs, te, src, dst, nt, w1, w3, w2, n_tiles_max, l):
    rows, d = xs.shape
    xs = xs.reshape(rows // ROW_CHUNK, ROW_CHUNK, d)
    grid_spec = pltpu.PrefetchScalarGridSpec(
        num_scalar_prefetch=4,
        grid=(n_tiles_max,),
        in_specs=[pl.BlockSpec(memory_space=pl.ANY),
                  pl.BlockSpec((None, None, d, D_EXPERT), lambda j, te, *_: (l, te[j], 0, 0)),
                  pl.BlockSpec((None, None, d, D_EXPERT), lambda j, te, *_: (l, te[j], 0, 0)),
                  pl.BlockSpec((None, None, D_EXPERT, d), lambda j, te, *_: (l, te[j], 0, 0))],
        out_specs=pl.BlockSpec(memory_space=pl.ANY),
        scratch_shapes=[pltpu.VMEM((NBUF, CPT, ROW_CHUNK, d), BF16),
                        pltpu.VMEM((NBUF, CPT, ROW_CHUNK, d), BF16),
                        pltpu.VMEM((d, D_EXPERT), BF16),
                        pltpu.VMEM((d, D_EXPERT), BF16),
                        pltpu.VMEM((D_EXPERT, d), BF16),
                        pltpu.SemaphoreType.DMA((NBUF,)),
                        pltpu.SemaphoreType.DMA((NBUF,))])
    return pl.pallas_call(
        _expert_kernel,
        grid_spec=grid_spec,
        out_shape=jax.ShapeDtypeStruct(xs.shape, xs.dtype),
        input_output_aliases={4: 0},
        compiler_params=pltpu.CompilerParams(
            dimension_semantics=("arbitrary",), vmem_limit_bytes=VMEM_LIMIT),
        name="experts",
    )(te, src, dst, nt, xs, w1, w3, w2).reshape(rows, d)


def _tile_plan(nch_f, n_tiles_max):
    nch = nch_f[:-RB_BLOCKS, :, 0]
    nb = nch.shape[0]
    hi = lax.Precision.HIGHEST
    choff = jnp.cumsum(nch, axis=1) - nch
    per_expert = jnp.sum(nch, axis=0)
    ntile = jnp.floor((per_expert + (CPT - 1.0)) * (1.0 / CPT))
    tile_end = jnp.cumsum(ntile)
    tile_start = tile_end - ntile
    nt = tile_end[-1]
    jt = jnp.arange(n_tiles_max, dtype=F32)
    te = jnp.minimum(jnp.sum((tile_end[None, :] <= jt[:, None]).astype(F32), axis=1), N_EXPERTS - 1.0)
    oh = (te[:, None] == jnp.arange(N_EXPERTS, dtype=F32)[None, :]).astype(F32)
    cum_t = jnp.dot(oh, jnp.cumsum(nch, axis=0).T, precision=hi)
    choff_t = jnp.dot(oh, choff.T, precision=hi)
    start_t = jnp.dot(oh, tile_start, precision=hi)
    total_t = jnp.dot(oh, per_expert, precision=hi)
    q = (jt - start_t)[:, None] * CPT + jnp.arange(CPT, dtype=F32)[None, :]
    le = cum_t[:, None, :] <= q[:, :, None]
    b = jnp.minimum(jnp.sum(le.astype(F32), axis=2), nb - 1.0)
    before = jnp.max(jnp.where(le, cum_t[:, None, :], 0.0), axis=2)
    is_b = jnp.arange(nb, dtype=F32)[None, None, :] == b[:, :, None]
    choff_s = jnp.sum(jnp.where(is_b, choff_t[:, None, :], 0.0), axis=2)
    valid = (jt[:, None] < nt) & (q < total_t[:, None])
    cid = b * (R_BLK // ROW_CHUNK) + choff_s + (q - before)
    assert (NBUF + 1) * CPT <= R_BLK // ROW_CHUNK
    lane = jnp.arange(CPT, dtype=F32)[None, :]
    spare0 = nb * (R_BLK // ROW_CHUNK)
    spare = spare0 + jnp.mod(jt, float(NBUF))[:, None] * CPT + lane
    zero_chunk = spare0 + NBUF * CPT + lane
    src = jnp.where(valid, cid, zero_chunk).astype(jnp.int32).reshape(-1)
    dst = jnp.concatenate([spare0 + (NBUF - 1) * CPT + lane, jnp.where(valid, cid, spare)],
                          axis=0).astype(jnp.int32).reshape(-1)
    return te.astype(jnp.int32), src, dst, nt.astype(jnp.int32).reshape(1)


def _combine_kernel(x_ref, mod_ref, pg_ref, lh_ref, ys_ref, sw1_ref, sw3_ref, sw2_ref, lng_ref,
                    lnb_ref, o_ref, w_scr, *, alpha):
    x = x_ref[...]
    sh2 = mod_ref[3:4, :]
    sc2 = mod_ref[4:5, :]
    g2 = mod_ref[5:6, :]
    h2b = (x * (1.0 + sc2) + sh2).astype(BF16)
    hid = (_silu(_dot(h2b, sw1_ref[...])) * _dot(h2b, sw3_ref[...])).astype(BF16)
    ffn = _dot(hid, sw2_ref[...])
    routed = []
    for c in range(RB_BLOCKS):
        _sorted_row_matrix(pg_ref.at[c], lh_ref.at[c], w_scr.at[c], weighted=True)
        routed.append(_dot_tn(ys_ref[c * R_BLK:(c + 1) * R_BLK, :], w_scr[c]).T)
    ffn = ffn + jnp.concatenate(routed, axis=0)
    o_ref[...] = _layer_norm(alpha * x + g2 * ffn, lng_ref[...], lnb_ref[...])


def _combine(x1, mod, pg, lh, ys, sw1, sw3, sw2, lng, lnb, blocks_per_batch, alpha, l):
    t, d = x1.shape
    nb = t // TB
    rb = RB_BLOCKS
    assert nb % rb == 0 and blocks_per_batch % rb == 0
    return pl.pallas_call(
        functools.partial(_combine_kernel, alpha=alpha),
        grid=(nb // rb,),
        in_specs=[pl.BlockSpec((rb * TB, d), lambda i: (i, 0)),
                  pl.BlockSpec((None, None, 6, d),
                               lambda i: (l, i // (blocks_per_batch // rb), 0, 0)),
                  pl.BlockSpec((rb, 4 * N_EXPERTS, TB), lambda i: (i, 0, 0)),
                  pl.BlockSpec((rb, 8, 128), lambda i: (i, 0, 0)),
                  pl.BlockSpec((rb * R_BLK, d), lambda i: (i, 0)),
                  _layer(sw1, l), _layer(sw3, l), _layer(sw2, l), _layer(lng, l), _layer(lnb, l)],
        out_specs=pl.BlockSpec((rb * TB, d), lambda i: (i, 0)),
        out_shape=jax.ShapeDtypeStruct((t, d), F32),
        scratch_shapes=[pltpu.VMEM((rb, R_BLK, TB), BF16)],
        compiler_params=pltpu.CompilerParams(
            dimension_semantics=("arbitrary",), vmem_limit_bytes=VMEM_LIMIT),
        name="combine",
    )(x1, mod, pg, lh, ys, sw1, sw3, sw2, lng, lnb)


def kernel(x, c, w_ada, b_ada, w_in, lru_conv_w, lru_conv_b, lru_w_a, lru_b_a, lru_w_x, lru_b_x, lru_lambda, gla_w_alpha, gla_b_alpha, gla_norm, ssd_conv_w, ssd_conv_b, ssd_dt_bias, ssd_a_log, ssd_d, ssd_norm, w_out, ln1_g, ln1_b, router_w, router_bias, exp_w1, exp_w3, exp_w2, shared_w1, shared_w3, shared_w2, ln2_g, ln2_b):
    bsz, seq, d = x.shape
    n_layers = w_ada.shape[0]
    assert d == D_MODEL and seq % TS == 0 and seq % TB == 0
    alpha = float((2 * n_layers) ** 0.25)
    t = bsz * seq
    nb = t // TB
    blocks_per_batch = seq // TB
    n_tiles_max = -(-(nb * (R_BLK // ROW_CHUNK)) // CPT) + N_EXPERTS

    mod = _adaln(c, w_ada, b_ada).reshape(n_layers, bsz, 6, d)

    win = _relayout_w_in(w_in)
    eye = jnp.eye(LRU_BLOCKS, dtype=F32)
    bd = lambda w: jnp.einsum("lnij,nm->lnimj", w, eye).reshape(n_layers, LRU_WIDTH, LRU_WIDTH)
    wg = jnp.concatenate([bd(lru_w_a), bd(lru_w_x)], axis=-1).astype(BF16)
    bg = jnp.concatenate([lru_b_a, lru_b_x], axis=-1)[:, None, :]
    wal = jnp.concatenate([gla_w_alpha, jnp.zeros((n_layers, 128 - GLA_RANK, GLA_QK), F32)],
                          axis=1).astype(BF16)
    pad_dt = lambda v: jnp.pad(v, ((0, 0), (RD_DT0, 128 - RD_DT0 - SSD_HEADS)))[:, None, :]
    e_np = np.zeros((128, SSD_WIDTH), np.float32)
    for hh in range(SSD_HEADS):
        e_np[RD_DT0 + hh, hh * SSD_HEADDIM:(hh + 1) * SSD_HEADDIM] = 1.0
    e_mat = jnp.asarray(e_np, BF16)
    shift_mat = jnp.asarray(
        np.concatenate([np.eye(TS, k=-(CONV_W - 1 - k), dtype=np.float32)
                        for k in range(CONV_W - 1)], axis=0), BF16)
    wout = w_out.astype(BF16)
    rwt = jnp.swapaxes(router_w, 1, 2)
    ustrict = jnp.asarray(np.triu(np.ones((TB, TB), np.float32), 1), BF16)
    lstrict = jnp.asarray(np.tril(np.ones((N_EXPERTS, N_EXPERTS), np.float32), -1), BF16)
    sw1 = shared_w1.astype(BF16)
    sw3 = shared_w3.astype(BF16)
    sw2 = shared_w2.astype(BF16)

    row = lambda v: v[:, None, :]
    p = dict(win=win, lcw=lru_conv_w, lcb=row(lru_conv_b), wg=wg, bg=bg, lam=row(lru_lambda),
             wal=wal, bal=row(gla_b_alpha), gnorm=row(jnp.tile(gla_norm, (1, GLA_HEADS))),
             scw=ssd_conv_w, scb=row(ssd_conv_b), dtb=pad_dt(ssd_dt_bias), alog=pad_dt(ssd_a_log),
             dskip=row(jnp.repeat(ssd_d, SSD_HEADDIM, axis=1)), snorm=row(ssd_norm),
             e=e_mat, shift=shift_mat, wout=wout, lng=row(ln1_g), lnb=row(ln1_b))
    rbias = router_bias[:, :, None]
    ln2g, ln2b = row(ln2_g), row(ln2_b)

    for l in range(n_layers):
        x1 = _mixer(x, mod, p, alpha, l).reshape(t, d)
        xs, pg, nch, lh = _router(x1, mod, rwt, rbias, ustrict, lstrict, blocks_per_batch, l)
        te, src, dst, nt = _tile_plan(nch, n_tiles_max + NBUF - 1)
        ys = _experts(xs, te, src, dst, nt, exp_w1, exp_w3, exp_w2, n_tiles_max, l)
        x = _combine(x1, mod, pg, lh, ys, sw1, sw3, sw2, ln2g, ln2b, blocks_per_batch, alpha,
                     l).reshape(bsz, seq, d)
    return x
```

```python
import functools

import jax
import jax.numpy as jnp
import numpy as np
from jax import lax
from jax.experimental import pallas as pl
from jax.experimental.pallas import tpu as pltpu

F32 = jnp.float32
BF16 = jnp.bfloat16

D_MODEL = 1024
CHUNK = 64
LRU_WIDTH = 512
LRU_BLOCKS = 8
LRU_C = 8.0
CONV_W = 4
GLA_WIDTH = 512
GLA_HEADS = 4
GLA_DV = 128
GLA_DK = 64
GLA_QK = 256
GLA_RANK = 16
GLA_TAU = 16.0
SSD_WIDTH = 1024
SSD_HEADDIM = 64
SSD_HEADS = 16
SSD_GROUPS = 2
SSD_HPG = 8
SSD_STATE = 128
SSD_CONV_DIM = SSD_WIDTH + 2 * SSD_GROUPS * SSD_STATE
D_MIX = 2048
N_EXPERTS = 64
TOP_K = 8
N_EXPERT_GROUPS = 8
EXPERTS_PER_GROUP = 8
TOPK_GROUPS = 4
D_EXPERT = 256
ROUTE_SCALE = 2.5
LN_EPS = 1e-5
RMS_EPS = 1e-6

C_LRU_X = 0
C_LRU_G = 512
C_Q = 1024
C_K = 1280
C_V = 1536
C_G = 2048
C_Z = 2560
C_XBC = 3584
C_RD = 5120
D_IN_PAD = 5248
RD_DT0 = 16

TS = 256
MIX_ROWS = 2
N_MIX_WEIGHTS = 20
TB = 256
RB_BLOCKS = 2
ROW_CHUNK = 16
CPT = 32
TILE_M = ROW_CHUNK * CPT
NBUF = 8
R_BLK = -(-(TOP_K * TB + N_EXPERTS * (ROW_CHUNK - 1)) // 256) * 256
N_GRP = R_BLK // 256
VMEM_LIMIT = 56 * 1024 * 1024


def _dot(a, b):
    return jnp.dot(a, b, preferred_element_type=F32)


def _dot_nt(a, b):
    return lax.dot_general(a, b, (((1,), (1,)), ((), ())), preferred_element_type=F32)


def _dot_tn(a, b):
    return lax.dot_general(a, b, (((0,), (0,)), ((), ())), preferred_element_type=F32)


def _sigmoid(x):
    return jax.nn.sigmoid(x)


def _silu(x):
    return x * _sigmoid(x)


def _softplus(x):
    return jnp.maximum(x, 0.0) + jnp.log1p(jnp.exp(-jnp.abs(x)))


def _gelu_tanh(x):
    return 0.5 * x * (1.0 + jnp.tanh(0.7978845608028654 * (x + 0.044715 * (x * x * x))))


def _layer_norm(x, g, b):
    mu = jnp.mean(x, axis=-1, keepdims=True)
    xc = x - mu
    var = jnp.mean(xc * xc, axis=-1, keepdims=True)
    return xc * lax.rsqrt(var + LN_EPS) * g + b


def _shift_rows(x, d, fill, rowmod):
    return jnp.where(rowmod >= d, pltpu.roll(x, d, axis=0), fill)


def _cumsum_rows(x, period):
    rowmod = lax.broadcasted_iota(jnp.int32, x.shape, 0) & (period - 1)
    d = 1
    while d < period:
        x = x + _shift_rows(x, d, 0.0, rowmod)
        d *= 2
    return x


def _causal_conv(carry_ref, x, w_ref, b_ref, shift_ref):
    n = x.shape[0]
    delayed = _dot(shift_ref[...], x.astype(BF16))
    y = b_ref[...] + w_ref[CONV_W - 1:CONV_W, :] * x
    carry = carry_ref[...]
    row8 = lax.broadcasted_iota(jnp.int32, (8, 1), 0)
    head = jnp.zeros((8, x.shape[1]), F32)
    for k in range(CONV_W - 1):
        delay = CONV_W - 1 - k
        y = y + w_ref[k:k + 1, :] * delayed[k * n:(k + 1) * n, :]
        head = head + w_ref[k:k + 1, :] * jnp.where(row8 < delay, pltpu.roll(carry, delay, axis=0), 0.0)
    carry_ref[...] = x[n - 8:n, :]
    return jnp.concatenate([y[0:8, :] + head, y[8:n, :]], axis=0)


def _adaln_kernel(c_ref, w_ref, b_ref, o_ref):
    cond = _silu(c_ref[...])
    o_ref[...] = jnp.dot(cond, w_ref[...], preferred_element_type=F32,
                         precision=lax.Precision.HIGHEST) + b_ref[...]


def _adaln(c, w_ada, b_ada):
    n_layers, d, n = w_ada.shape
    bsz = c.shape[0]
    tn = 1024
    return pl.pallas_call(
        _adaln_kernel,
        grid=(n_layers, n // tn),
        in_specs=[pl.BlockSpec((bsz, d), lambda l, j: (0, 0)),
                  pl.BlockSpec((None, d, tn), lambda l, j: (l, 0, j)),
                  pl.BlockSpec((None, 1, tn), lambda l, j: (l, 0, j))],
        out_specs=pl.BlockSpec((None, bsz, tn), lambda l, j: (l, 0, j)),
        out_shape=jax.ShapeDtypeStruct((n_layers, bsz, n), F32),
        name="adaln",
    )(c, w_ada, b_ada.reshape(n_layers, 1, n))


O_R = 2 * LRU_WIDTH + 2 * GLA_QK + 2 * GLA_WIDTH
O_Z = O_R + GLA_RANK
O_DT = O_Z + SSD_WIDTH + SSD_CONV_DIM
D_IN_PROJ = O_DT + SSD_HEADS


def _relayout_kernel(w_ref, o_ref):
    o_ref[:, 0:O_R] = w_ref[:, 0:O_R].astype(BF16)
    o_ref[:, C_Z:C_RD] = w_ref[:, O_Z:O_DT].astype(BF16)
    tail = jnp.concatenate(
        [w_ref[:, O_R:O_Z], w_ref[:, O_DT:D_IN_PROJ],
         jnp.zeros((w_ref.shape[0], 128 - GLA_RANK - SSD_HEADS), F32)], axis=1)
    o_ref[:, C_RD:D_IN_PAD] = tail.astype(BF16)


def _relayout_w_in(w_in):
    n_layers, d, n = w_in.shape
    assert n == D_IN_PROJ and O_R == C_Z and C_RD - C_Z == O_DT - O_Z
    rows = 256
    return pl.pallas_call(
        _relayout_kernel,
        grid=(n_layers, d // rows),
        in_specs=[pl.BlockSpec((None, rows, n), lambda l, i: (l, i, 0))],
        out_specs=pl.BlockSpec((None, rows, D_IN_PAD), lambda l, i: (l, i, 0)),
        out_shape=jax.ShapeDtypeStruct((n_layers, d, D_IN_PAD), BF16),
        name="relayout_w_in",
    )(w_in)


def _mixer_kernel(x_ref, mod_ref, *refs, alpha):
    weights, o_ref, scratch = refs[:N_MIX_WEIGHTS], refs[N_MIX_WEIGHTS], refs[N_MIX_WEIGHTS + 1:]

    @pl.when(pl.program_id(1) == 0)
    def _():
        for s_ref in scratch[:5]:
            s_ref[...] = jnp.zeros(s_ref.shape, F32)

    _mixer_tile(x_ref, mod_ref, *weights, o_ref, *scratch, alpha=alpha)


def _mixer_tile(x_ref, mod_ref, win_ref, lcw_ref, lcb_ref, wg_ref, bg_ref, lam_ref,
                wal_ref, bal_ref, gnorm_ref, scw_ref, scb_ref, dtb_ref, alog_ref,
                dskip_ref, snorm_ref, e_ref, shift_ref, wout_ref, lng_ref, lnb_ref,
                o_ref,
                stage_l, stage_s, hcar, gla_st, ssd_st, o_scr, cat, *, alpha):
    chains = range(MIX_ROWS)
    rows_of = lambda c: slice(c * TS, (c + 1) * TS)
    stack = lambda parts: jnp.concatenate(parts, axis=0)
    per_chain = lambda fn: stack([jnp.broadcast_to(fn(c), (TS, fn(c).shape[1])) for c in chains])

    x = x_ref[...].reshape(MIX_ROWS * TS, D_MODEL)
    sh1 = per_chain(lambda c: mod_ref[c, 0:1, :])
    sc1 = per_chain(lambda c: mod_ref[c, 1:2, :])
    g1 = per_chain(lambda c: mod_ref[c, 2:3, :])
    h = (x * (1.0 + sc1) + sh1).astype(BF16)
    row = lax.broadcasted_iota(jnp.int32, (MIX_ROWS * TS, 1), 0) & (TS - 1)

    lx = _dot(h, win_ref[:, C_LRU_X:C_LRU_X + LRU_WIDTH])
    xc = stack([_causal_conv(stage_l.at[c], lx[rows_of(c)], lcw_ref, lcb_ref, shift_ref)
                for c in chains])
    gts = _dot(xc.astype(BF16), wg_ref[...]) + bg_ref[...]
    r = _sigmoid(gts[:, :LRU_WIDTH])
    i_g = _sigmoid(gts[:, LRU_WIDTH:])
    log_a = (-LRU_C) * r * _softplus(-lam_ref[...])
    a = jnp.exp(log_a)
    u = jnp.sqrt(-jnp.tanh(log_a) * (a * a + 1.0)) * (i_g * xc)
    row8 = row & 7
    acc_a, acc_h = a, u
    d = 1
    while d < 8:
        sh_h = jnp.where(row8 >= d, pltpu.roll(acc_h, d, axis=0), 0.0)
        sh_a = jnp.where(row8 >= d, pltpu.roll(acc_a, d, axis=0), 1.0)
        acc_h = acc_h + acc_a * sh_h
        acc_a = acc_a * sh_a
        d *= 2
    groups = []
    for c in chains:
        carry = hcar[c, 0:1, :]
        for g in range(TS // 8):
            r0 = c * TS + 8 * g
            hg = acc_h[r0:r0 + 8, :] + acc_a[r0:r0 + 8, :] * carry
            carry = hg[7:8, :]
            groups.append(hg)
        hcar[c, 0:1, :] = carry
    h_lru = stack(groups)
    lgate = _dot(h, win_ref[:, C_LRU_G:C_LRU_G + LRU_WIDTH])
    cat[:, 0:LRU_WIDTH] = (h_lru * _gelu_tanh(lgate)).astype(BF16)

    rd = _dot(h, win_ref[:, C_RD:C_RD + 128])

    q = _dot(h, win_ref[:, C_Q:C_Q + GLA_QK]) * (GLA_DK ** -0.5)
    k = _dot(h, win_ref[:, C_K:C_K + GLA_QK])
    v = _dot(h, win_ref[:, C_V:C_V + GLA_WIDTH]).astype(BF16)
    zal = _dot(rd.astype(BF16), wal_ref[...]) + bal_ref[...]
    log_alpha = (jnp.minimum(zal, 0.0) - jnp.log1p(jnp.exp(-jnp.abs(zal)))) * (1.0 / GLA_TAU)
    cum = _cumsum_rows(log_alpha, CHUNK)
    q_dec = q * jnp.exp(cum)
    k_dec = (k * jnp.exp(-cum)).astype(BF16)
    ri = lax.broadcasted_iota(jnp.int32, (TS, TS), 0)
    ci = lax.broadcasted_iota(jnp.int32, (TS, TS), 1)
    causal = ri >= ci
    blk_causal = causal & ((ri // CHUNK) == (ci // CHUNK))
    lane_head = lax.broadcasted_iota(jnp.int32, (1, GLA_QK), 1) // GLA_DK
    for hd in range(GLA_HEADS):
        qm = jnp.where(lane_head == hd, q_dec, 0.0).astype(BF16)
        for c in chains:
            att = jnp.where(blk_causal, _dot_nt(qm[rows_of(c)], k_dec[rows_of(c)]), 0.0)
            o_scr[rows_of(c), hd * GLA_DV:(hd + 1) * GLA_DV] = _dot(
                att.astype(BF16), v[rows_of(c), hd * GLA_DV:(hd + 1) * GLA_DV])
    sr = lax.broadcasted_iota(jnp.int32, (GLA_WIDTH, GLA_QK), 0) // GLA_DV
    scol = lax.broadcasted_iota(jnp.int32, (GLA_WIDTH, GLA_QK), 1) // GLA_DK
    st_mask = sr == scol
    q_dec_b = q_dec.astype(BF16)
    for c in chains:
        for ch in range(TS // CHUNK):
            rows = slice(c * TS + ch * CHUNK, c * TS + (ch + 1) * CHUNK)
            st = gla_st[c]
            o_scr[rows, :] = o_scr[rows, :] + _dot_nt(q_dec_b[rows, :], st.astype(BF16))
            cum_c = cum[rows, :]
            cum_last = cum_c[CHUNK - 1:CHUNK, :]
            k_end = (k[rows, :] * jnp.exp(cum_last - cum_c)).astype(BF16)
            upd = _dot_tn(v[rows, :], k_end)
            gla_st[c] = st * jnp.exp(cum_last) + jnp.where(st_mask, upd, 0.0)
    g_gate = _dot(h, win_ref[:, C_G:C_G + GLA_WIDTH])
    for hd in range(GLA_HEADS):
        cols = slice(hd * GLA_DV, (hd + 1) * GLA_DV)
        oh = o_scr[:, cols]
        ms = jnp.mean(oh * oh, axis=-1, keepdims=True)
        on = oh * lax.rsqrt(ms + RMS_EPS) * gnorm_ref[:, cols]
        cat[:, LRU_WIDTH + hd * GLA_DV:LRU_WIDTH + (hd + 1) * GLA_DV] = (
            on * _silu(g_gate[:, cols])).astype(BF16)

    xbc = _dot(h, win_ref[:, C_XBC:C_XBC + SSD_CONV_DIM])
    xbc = _silu(stack([_causal_conv(stage_s.at[c], xbc[rows_of(c)], scw_ref, scb_ref, shift_ref)
                       for c in chains]))
    xs = xbc[:, :SSD_WIDTH]
    lane128 = lax.broadcasted_iota(jnp.int32, (1, 128), 1)
    dt_lane = (lane128 >= RD_DT0) & (lane128 < RD_DT0 + SSD_HEADS)
    a_neg = jnp.where(dt_lane, -jnp.exp(alog_ref[...]), 0.0)
    dtp = _softplus(rd + dtb_ref[...])
    a_cs = _cumsum_rows(dtp * a_neg, TS)
    a_last = per_chain(lambda c: a_cs[(c + 1) * TS - 1:(c + 1) * TS, :])
    ea = jnp.exp(a_cs)
    dte = jnp.exp(a_last - a_cs)
    n_rows = MIX_ROWS * TS
    expand = _dot(jnp.concatenate([dtp, dte, ea], axis=0).astype(BF16), e_ref[...])
    dt_x = expand[0:n_rows, :]
    dte_x = expand[n_rows:2 * n_rows, :]
    ea_x = expand[2 * n_rows:3 * n_rows, :]
    xdt = xs * dt_x
    xdt_b = xdt.astype(BF16)
    xdte_b = (xdt * dte_x).astype(BF16)
    a_cs_t = a_cs.T
    z = _dot(h, win_ref[:, C_Z:C_Z + SSD_WIDTH])
    lane_lo = lax.broadcasted_iota(jnp.int32, (1, 128), 1) < SSD_HEADDIM
    gw = SSD_HPG * SSD_HEADDIM
    for g in range(SSD_GROUPS):
        gcols = slice(g * gw, (g + 1) * gw)
        bm = xbc[:, SSD_WIDTH + g * SSD_STATE:SSD_WIDTH + (g + 1) * SSD_STATE].astype(BF16)
        cm = xbc[:, SSD_WIDTH + (SSD_GROUPS + g) * SSD_STATE:
                 SSD_WIDTH + (SSD_GROUPS + g + 1) * SSD_STATE].astype(BF16)
        y_chain = []
        for c in chains:
            rc = rows_of(c)
            cb = _dot_nt(cm[rc], bm[rc])
            st = ssd_st[c, g]
            y_off = _dot(cm[rc], st.astype(BF16)) * ea_x[rc, gcols]
            y_pairs = []
            for j in range(SSD_HPG // 2):
                ys = []
                for hh in (2 * j, 2 * j + 1):
                    lane = RD_DT0 + g * SSD_HPG + hh
                    diff = a_cs[rc, lane:lane + 1] - a_cs_t[lane:lane + 1, rc]
                    m = (cb * jnp.exp(jnp.where(causal, diff, -jnp.inf))).astype(BF16)
                    ys.append(_dot(m, xdt_b[rc, g * gw + j * 128:g * gw + (j + 1) * 128]))
                y_pairs.append(jnp.where(lane_lo, ys[0], ys[1]) + y_off[:, j * 128:(j + 1) * 128])
            y_chain.append(y_pairs)
            ssd_st[c, g] = (st * ea_x[(c + 1) * TS - 1:(c + 1) * TS, gcols]
                            + _dot_tn(bm[rc], xdte_b[rc, gcols]))
        yg = []
        for j in range(SSD_HPG // 2):
            cols = slice(g * gw + j * 128, g * gw + (j + 1) * 128)
            y = stack([y_chain[c][j] for c in chains]) + xs[:, cols] * dskip_ref[:, cols]
            yg.append(y * _silu(z[:, cols]))
        ssq = yg[0] * yg[0]
        for t in yg[1:]:
            ssq = ssq + t * t
        ms = jnp.sum(ssq, axis=-1, keepdims=True) * (1.0 / gw)
        inv = lax.rsqrt(ms + RMS_EPS)
        for j, t in enumerate(yg):
            cols = slice(g * gw + j * 128, g * gw + (j + 1) * 128)
            cat[:, LRU_WIDTH + GLA_WIDTH + g * gw + j * 128:
                LRU_WIDTH + GLA_WIDTH + g * gw + (j + 1) * 128] = (
                    t * inv * snorm_ref[:, cols]).astype(BF16)

    mix = _dot(cat[...], wout_ref[...])
    out = _layer_norm(alpha * x + g1 * mix, lng_ref[...], lnb_ref[...])
    o_ref[...] = out.reshape(MIX_ROWS, TS, D_MODEL)


def _full(shape):
    nd = len(shape)
    return pl.BlockSpec(shape, lambda *_: (0,) * nd)


def _layer(arr, l):
    nd = arr.ndim
    return pl.BlockSpec((None,) + arr.shape[1:], lambda *_: (l,) + (0,) * (nd - 1))


def _mixer(x, mod, p, alpha, l):
    bsz, seq, d = x.shape
    names = ["win", "lcw", "lcb", "wg", "bg", "lam", "wal", "bal", "gnorm", "scw", "scb", "dtb",
             "alog", "dskip", "snorm", "e", "shift", "wout", "lng", "lnb"]
    weights = [p[n] for n in names]
    assert bsz % MIX_ROWS == 0 and len(names) == N_MIX_WEIGHTS
    in_specs = [pl.BlockSpec((MIX_ROWS, TS, d), lambda b, s: (b, s, 0)),
                pl.BlockSpec((None, MIX_ROWS, 6, d), lambda b, s: (l, b, 0, 0))]
    in_specs += [_full(w.shape) if n in ("e", "shift") else _layer(w, l)
                 for n, w in zip(names, weights)]
    return pl.pallas_call(
        functools.partial(_mixer_kernel, alpha=alpha),
        grid=(bsz // MIX_ROWS, seq // TS),
        in_specs=in_specs,
        out_specs=pl.BlockSpec((MIX_ROWS, TS, d), lambda b, s: (b, s, 0)),
        out_shape=jax.ShapeDtypeStruct((bsz, seq, d), F32),
        scratch_shapes=[pltpu.VMEM((MIX_ROWS, 8, LRU_WIDTH), F32),
                        pltpu.VMEM((MIX_ROWS, 8, SSD_CONV_DIM), F32),
                        pltpu.VMEM((MIX_ROWS, 8, LRU_WIDTH), F32),
                        pltpu.VMEM((MIX_ROWS, GLA_WIDTH, GLA_QK), F32),
                        pltpu.VMEM((MIX_ROWS, SSD_GROUPS, SSD_STATE, SSD_HPG * SSD_HEADDIM), F32),
                        pltpu.VMEM((MIX_ROWS * TS, GLA_WIDTH), F32),
                        pltpu.VMEM((MIX_ROWS * TS, D_MIX), BF16)],
        compiler_params=pltpu.CompilerParams(
            dimension_semantics=("arbitrary", "arbitrary"), vmem_limit_bytes=VMEM_LIMIT),
        name="mixer",
    )(x, mod, *weights)


def _router_kernel(*refs):
    last = pl.program_id(0) == pl.num_programs(0) - 1

    @pl.when(jnp.logical_not(last))
    def _():
        x_ref, mod_ref, rwt_ref, rbias_ref, ustrict_ref, lstrict_ref = refs[:6]
        xs_ref, pg_ref, nch_ref, lh_ref, msk_scr, sq_scr, p_scr = refs[6:]
        for c in range(RB_BLOCKS):
            _route_block(x_ref.at[pl.ds(c * TB, TB)], mod_ref, rwt_ref, rbias_ref, ustrict_ref,
                         lstrict_ref, xs_ref.at[pl.ds(c * R_BLK, R_BLK)], pg_ref.at[c],
                         nch_ref.at[c], lh_ref.at[c], msk_scr.at[c], sq_scr.at[c], p_scr.at[c])

    @pl.when(last)
    def _():
        for o_ref in refs[6:10]:
            o_ref[...] = jnp.zeros(o_ref.shape, o_ref.dtype)


def _route_block(x_ref, mod_ref, rwt_ref, rbias_ref, ustrict_ref, lstrict_ref,
                 xs_ref, pg_ref, nch_ref, lh_ref, msk_scr, sq_scr, p_scr):
    x = x_ref[...]
    sh2 = mod_ref[3:4, :]
    sc2 = mod_ref[4:5, :]
    h2 = x * (1.0 + sc2) + sh2
    logits = lax.dot_general(rwt_ref[...], h2, (((1,), (1,)), ((), ())),
                             preferred_element_type=F32, precision=lax.Precision.HIGHEST)
    scores = _sigmoid(logits)
    biased = scores + rbias_ref[...]

    row8 = lax.broadcasted_iota(jnp.int32, (EXPERTS_PER_GROUP, TB), 0)
    gsc = []
    for g in range(N_EXPERT_GROUPS):
        vals = biased[g * EXPERTS_PER_GROUP:(g + 1) * EXPERTS_PER_GROUP, :]
        m1 = jnp.max(vals, axis=0, keepdims=True)
        i1 = jnp.min(jnp.where(vals == m1, row8, EXPERTS_PER_GROUP), axis=0, keepdims=True)
        m2 = jnp.max(jnp.where(row8 == i1, -jnp.inf, vals), axis=0, keepdims=True)
        gsc.append(m1 + m2)
    for g in range(N_EXPERT_GROUPS):
        beaten = jnp.zeros((1, TB), F32)
        for o in range(N_EXPERT_GROUPS):
            if o == g:
                continue
            wins = (gsc[o] >= gsc[g]) if o < g else (gsc[o] > gsc[g])
            beaten = beaten + jnp.where(wins, 1.0, 0.0)
        keep = beaten < TOPK_GROUPS
        vals = biased[g * EXPERTS_PER_GROUP:(g + 1) * EXPERTS_PER_GROUP, :]
        msk_scr[g * EXPERTS_PER_GROUP:(g + 1) * EXPERTS_PER_GROUP, :] = jnp.where(keep, vals, -jnp.inf)

    cur = msk_scr[...]
    row64 = lax.broadcasted_iota(jnp.int32, (N_EXPERTS, TB), 0)
    sel = jnp.zeros((N_EXPERTS, TB), F32)
    for _ in range(TOP_K):
        m = jnp.max(cur, axis=0, keepdims=True)
        ik = jnp.min(jnp.where(cur == m, row64, N_EXPERTS), axis=0, keepdims=True)
        hit = row64 == ik
        sel = jnp.where(hit, 1.0, sel)
        cur = jnp.where(hit, -jnp.inf, cur)
    wsel = sel * scores
    gates = wsel * (ROUTE_SCALE / jnp.sum(wsel, axis=0, keepdims=True))

    rank = _dot(sel.astype(BF16), ustrict_ref[...])
    cnt = jnp.sum(sel, axis=1, keepdims=True)
    nch = jnp.floor((cnt + (ROW_CHUNK - 1.0)) * (1.0 / ROW_CHUNK))
    nch_b = jnp.broadcast_to(nch, (N_EXPERTS, TB))
    choff = _dot(lstrict_ref[...], nch_b.astype(BF16))
    lo = choff * float(ROW_CHUNK)
    posm = jnp.where(sel > 0.0, lo + rank, -1.0)
    nch_ref[...] = nch_b[:, 0:128]

    sq_scr[0:N_EXPERTS, :] = lo[:, 0:128]
    sq_scr[N_EXPERTS:2 * N_EXPERTS, :] = (lo + nch_b * float(ROW_CHUNK))[:, 0:128]
    lh_ref[...] = sq_scr[...].T[0:8, :]

    pg_ref[0:N_EXPERTS, :] = posm
    pg_ref[N_EXPERTS:2 * N_EXPERTS, :] = posm
    pg_ref[2 * N_EXPERTS:3 * N_EXPERTS, :] = gates
    pg_ref[3 * N_EXPERTS:4 * N_EXPERTS, :] = gates

    _sorted_row_matrix(pg_ref, lh_ref, p_scr, weighted=False)
    h2b = h2.astype(BF16)
    for n in range(D_MODEL // 256):
        cols = slice(n * 256, (n + 1) * 256)
        xs_ref[:, cols] = _dot(p_scr[...], h2b[:, cols]).astype(BF16)


def _sorted_row_matrix(pg_ref, lh_ref, out_scr, weighted):
    posm2 = pg_ref[0:2 * N_EXPERTS, :]
    pos_hi = jnp.floor(posm2 * (1.0 / 256.0))
    pos_lo1 = posm2 - pos_hi * 256.0 + 1.0
    lh_row = lh_ref[0:1, :]
    sign = jnp.where(lax.broadcasted_iota(jnp.int32, (1, 2 * N_EXPERTS), 1) < N_EXPERTS, 1.0, -1.0)
    r_l = lax.broadcasted_iota(jnp.int32, (256, 2 * N_EXPERTS), 0).astype(F32)
    r_t1 = lax.broadcasted_iota(jnp.int32, (256, TB), 0).astype(F32) + 1.0
    if weighted:
        gates2 = pg_ref[2 * N_EXPERTS:4 * N_EXPERTS, :].astype(BF16)
    for grp in range(N_GRP):
        owner = jnp.where(r_l + float(grp * 256) >= lh_row, sign, 0.0).astype(BF16)
        low = jnp.where(pos_hi == float(grp), pos_lo1, 0.0).astype(BF16)
        hit = _dot(owner, low) == r_t1
        val = _dot(owner, gates2) if weighted else 1.0
        out_scr[grp * 256:(grp + 1) * 256, :] = jnp.where(hit, val, 0.0).astype(BF16)


def _router(x1, mod, rwt, rbias, ustrict, lstrict, blocks_per_batch, l):
    t, d = x1.shape
    nb = t // TB
    rb = RB_BLOCKS
    assert nb % rb == 0 and blocks_per_batch % rb == 0
    steps = nb // rb
    return pl.pallas_call(
        _router_kernel,
        grid=(steps + 1,),
        in_specs=[pl.BlockSpec((rb * TB, d), lambda i: (jnp.minimum(i, steps - 1), 0)),
                  pl.BlockSpec((None, None, 6, d),
                               lambda i: (l, jnp.minimum(i, steps - 1) // (blocks_per_batch // rb),
                                          0, 0)),
                  _layer(rwt, l), _layer(rbias, l), _full(ustrict.shape), _full(lstrict.shape)],
        out_specs=[pl.BlockSpec((rb * R_BLK, d), lambda i: (i, 0)),
                   pl.BlockSpec((rb, 4 * N_EXPERTS, TB), lambda i: (i, 0, 0)),
                   pl.BlockSpec((rb, N_EXPERTS, 128), lambda i: (i, 0, 0)),
                   pl.BlockSpec((rb, 8, 128), lambda i: (i, 0, 0))],
        out_shape=[jax.ShapeDtypeStruct(((nb + rb) * R_BLK, d), BF16),
                   jax.ShapeDtypeStruct((nb + rb, 4 * N_EXPERTS, TB), F32),
                   jax.ShapeDtypeStruct((nb + rb, N_EXPERTS, 128), F32),
                   jax.ShapeDtypeStruct((nb + rb, 8, 128), F32)],
        scratch_shapes=[pltpu.VMEM((rb, N_EXPERTS, TB), F32), pltpu.VMEM((rb, 128, 128), F32),
                        pltpu.VMEM((rb, R_BLK, TB), BF16)],
        compiler_params=pltpu.CompilerParams(
            dimension_semantics=("arbitrary",), vmem_limit_bytes=VMEM_LIMIT),
        name="router",
    )(x1, mod, rwt, rbias, ustrict, lstrict)


def _expert_kernel(te_ref, src_ref, dst_ref, nt_ref, xs_hbm, w1_ref, w3_ref, w2_ref, ys_hbm,
                   xbuf, ybuf, w1b, w3b, w2b, sem_in, sem_out):
    j = pl.program_id(0)
    nt = nt_ref[0]
    slot = lax.rem(j, NBUF)

    def in_copy(cid, sl, i):
        return pltpu.make_async_copy(xs_hbm.at[cid], xbuf.at[sl, i], sem_in.at[sl])

    def out_copy(cid, sl, i):
        return pltpu.make_async_copy(ybuf.at[sl, i], ys_hbm.at[cid], sem_out.at[sl])

    def start_in(tile, sl):
        for i in range(CPT):
            in_copy(src_ref[tile * CPT + i], sl, i).start()

    def wait_in(sl):
        for i in range(CPT):
            in_copy(0, sl, i).wait()

    def start_out(tile, sl):
        for i in range(CPT):
            out_copy(dst_ref[(tile + 1) * CPT + i], sl, i).start(priority=1)

    def wait_out(sl):
        for i in range(CPT):
            out_copy(0, sl, i).wait()

    @pl.when(j == 0)
    def _():
        ybuf[NBUF - 1] = jnp.zeros(ybuf.shape[1:], BF16)
        for t in range(NBUF - 1):
            start_in(t, t)

    @pl.when(j < nt)
    def _():
        wait_in(slot)
        prev = te_ref[jnp.maximum(j - 1, 0)]

        @pl.when((j == 0) | (te_ref[j] != prev))
        def _():
            w1b[...] = w1_ref[...].astype(BF16)
            w3b[...] = w3_ref[...].astype(BF16)
            w2b[...] = w2_ref[...].astype(BF16)

        @pl.when(j >= NBUF - 1)
        def _():
            wait_out(slot)

        xt = xbuf[slot].reshape(TILE_M, D_MODEL)
        gate = _dot(xt, w1b[...])
        up = _dot(xt, w3b[...])
        ahead = j + (NBUF - 1)
        start_in(ahead, lax.rem(ahead, NBUF))
        start_out(j - 1, lax.rem(ahead, NBUF))
        hid = (_silu(gate) * up).astype(BF16)
        ybuf[slot] = _dot(hid, w2b[...]).astype(BF16).reshape(CPT, ROW_CHUNK, D_MODEL)

        @pl.when(j == nt - 1)
        def _():
            start_out(j, slot)
            for back in range(NBUF):
                @pl.when(j + 1 >= back)
                def _():
                    wait_out(lax.rem(j + NBUF - back, NBUF))
            for fwd in range(1, NBUF):
                wait_in(lax.rem(j + fwd, NBUF))


def _experts(xs, te, src, dst, nt, w1, w3, w2, n_tiles_max, l):
    rows, d = xs.shape
    xs = xs.reshape(rows // ROW_CHUNK, ROW_CHUNK, d)
    grid_spec = pltpu.PrefetchScalarGridSpec(
        num_scalar_prefetch=4,
        grid=(n_tiles_max,),
        in_specs=[pl.BlockSpec(memory_space=pl.ANY),
                  pl.BlockSpec((None, None, d, D_EXPERT), lambda j, te, *_: (l, te[j], 0, 0)),
                  pl.BlockSpec((None, None, d, D_EXPERT), lambda j, te, *_: (l, te[j], 0, 0)),
                  pl.BlockSpec((None, None, D_EXPERT, d), lambda j, te, *_: (l, te[j], 0, 0))],
        out_specs=pl.BlockSpec(memory_space=pl.ANY),
        scratch_shapes=[pltpu.VMEM((NBUF, CPT, ROW_CHUNK, d), BF16),
                        pltpu.VMEM((NBUF, CPT, ROW_CHUNK, d), BF16),
                        pltpu.VMEM((d, D_EXPERT), BF16),
                        pltpu.VMEM((d, D_EXPERT), BF16),
                        pltpu.VMEM((D_EXPERT, d), BF16),
                        pltpu.SemaphoreType.DMA((NBUF,)),
                        pltpu.SemaphoreType.DMA((NBUF,))])
    return pl.pallas_call(
        _expert_kernel,
        grid_spec=grid_spec,
        out_shape=jax.ShapeDtypeStruct(xs.shape, xs.dtype),
        input_output_aliases={4: 0},
        compiler_params=pltpu.CompilerParams(
            dimension_semantics=("arbitrary",), vmem_limit_bytes=VMEM_LIMIT),
        name="experts",
    )(te, src, dst, nt, xs, w1, w3, w2).reshape(rows, d)


def _tile_plan(nch_f, n_tiles_max):
    nch = nch_f[:-RB_BLOCKS, :, 0]
    nb = nch.shape[0]
    hi = lax.Precision.HIGHEST
    choff = jnp.cumsum(nch, axis=1) - nch
    per_expert = jnp.sum(nch, axis=0)
    ntile = jnp.floor((per_expert + (CPT - 1.0)) * (1.0 / CPT))
    tile_end = jnp.cumsum(ntile)
    tile_start = tile_end - ntile
    nt = tile_end[-1]
    jt = jnp.arange(n_tiles_max, dtype=F32)
    te = jnp.minimum(jnp.sum((tile_end[None, :] <= jt[:, None]).astype(F32), axis=1), N_EXPERTS - 1.0)
    oh = (te[:, None] == jnp.arange(N_EXPERTS, dtype=F32)[None, :]).astype(F32)
    cum_t = jnp.dot(oh, jnp.cumsum(nch, axis=0).T, precision=hi)
    choff_t = jnp.dot(oh, choff.T, precision=hi)
    start_t = jnp.dot(oh, tile_start, precision=hi)
    total_t = jnp.dot(oh, per_expert, precision=hi)
    q = (jt - start_t)[:, None] * CPT + jnp.arange(CPT, dtype=F32)[None, :]
    le = cum_t[:, None, :] <= q[:, :, None]
    b = jnp.minimum(jnp.sum(le.astype(F32), axis=2), nb - 1.0)
    before = jnp.max(jnp.where(le, cum_t[:, None, :], 0.0), axis=2)
    is_b = jnp.arange(nb, dtype=F32)[None, None, :] == b[:, :, None]
    choff_s = jnp.sum(jnp.where(is_b, choff_t[:, None, :], 0.0), axis=2)
    valid = (jt[:, None] < nt) & (q < total_t[:, None])
    cid = b * (R_BLK // ROW_CHUNK) + choff_s + (q - before)
    assert (NBUF + 1) * CPT <= RB_BLOCKS * (R_BLK // ROW_CHUNK)
    lane = jnp.arange(CPT, dtype=F32)[None, :]
    spare0 = nb * (R_BLK // ROW_CHUNK)
    spare = spare0 + jnp.mod(jt, float(NBUF))[:, None] * CPT + lane
    zero_chunk = spare0 + NBUF * CPT + lane
    src = jnp.where(valid, cid, zero_chunk).astype(jnp.int32).reshape(-1)
    dst = jnp.concatenate([spare0 + (NBUF - 1) * CPT + lane, jnp.where(valid, cid, spare)],
                          axis=0).astype(jnp.int32).reshape(-1)
    return te.astype(jnp.int32), src, dst, nt.astype(jnp.int32).reshape(1)


def _combine_kernel(x_ref, mod_ref, pg_ref, lh_ref, ys_ref, sw1_ref, sw3_ref, sw2_ref, lng_ref,
                    lnb_ref, o_ref, w_scr, *, alpha):
    x = x_ref[...]
    sh2 = mod_ref[3:4, :]
    sc2 = mod_ref[4:5, :]
    g2 = mod_ref[5:6, :]
    h2b = (x * (1.0 + sc2) + sh2).astype(BF16)
    hid = (_silu(_dot(h2b, sw1_ref[...])) * _dot(h2b, sw3_ref[...])).astype(BF16)
    ffn = _dot(hid, sw2_ref[...])
    routed = []
    for c in range(RB_BLOCKS):
        _sorted_row_matrix(pg_ref.at[c], lh_ref.at[c], w_scr.at[c], weighted=True)
        routed.append(_dot_tn(ys_ref[c * R_BLK:(c + 1) * R_BLK, :], w_scr[c]).T)
    ffn = ffn + jnp.concatenate(routed, axis=0)
    o_ref[...] = _layer_norm(alpha * x + g2 * ffn, lng_ref[...], lnb_ref[...])


def _combine(x1, mod, pg, lh, ys, sw1, sw3, sw2, lng, lnb, blocks_per_batch, alpha, l):
    t, d = x1.shape
    nb = t // TB
    rb = RB_BLOCKS
    assert nb % rb == 0 and blocks_per_batch % rb == 0
    return pl.pallas_call(
        functools.partial(_combine_kernel, alpha=alpha),
        grid=(nb // rb,),
        in_specs=[pl.BlockSpec((rb * TB, d), lambda i: (i, 0)),
                  pl.BlockSpec((None, None, 6, d),
                               lambda i: (l, i // (blocks_per_batch // rb), 0, 0)),
                  pl.BlockSpec((rb, 4 * N_EXPERTS, TB), lambda i: (i, 0, 0)),
                  pl.BlockSpec((rb, 8, 128), lambda i: (i, 0, 0)),
                  pl.BlockSpec((rb * R_BLK, d), lambda i: (i, 0)),
                  _layer(sw1, l), _layer(sw3, l), _layer(sw2, l), _layer(lng, l), _layer(lnb, l)],
        out_specs=pl.BlockSpec((rb * TB, d), lambda i: (i, 0)),
        out_shape=jax.ShapeDtypeStruct((t, d), F32),
        scratch_shapes=[pltpu.VMEM((rb, R_BLK, TB), BF16)],
        compiler_params=pltpu.CompilerParams(
            dimension_semantics=("arbitrary",), vmem_limit_bytes=VMEM_LIMIT),
        name="combine",
    )(x1, mod, pg, lh, ys, sw1, sw3, sw2, lng, lnb)


def kernel(x, c, w_ada, b_ada, w_in, lru_conv_w, lru_conv_b, lru_w_a, lru_b_a, lru_w_x, lru_b_x, lru_lambda, gla_w_alpha, gla_b_alpha, gla_norm, ssd_conv_w, ssd_conv_b, ssd_dt_bias, ssd_a_log, ssd_d, ssd_norm, w_out, ln1_g, ln1_b, router_w, router_bias, exp_w1, exp_w3, exp_w2, shared_w1, shared_w3, shared_w2, ln2_g, ln2_b):
    bsz, seq, d = x.shape
    n_layers = w_ada.shape[0]
    assert d == D_MODEL and seq % TS == 0 and seq % TB == 0
    alpha = float((2 * n_layers) ** 0.25)
    t = bsz * seq
    nb = t // TB
    blocks_per_batch = seq // TB
    n_tiles_max = -(-(nb * (R_BLK // ROW_CHUNK)) // CPT) + N_EXPERTS

    mod = _adaln(c, w_ada, b_ada).reshape(n_layers, bsz, 6, d)

    win = _relayout_w_in(w_in)
    eye = jnp.eye(LRU_BLOCKS, dtype=F32)
    bd = lambda w: jnp.einsum("lnij,nm->lnimj", w, eye).reshape(n_layers, LRU_WIDTH, LRU_WIDTH)
    wg = jnp.concatenate([bd(lru_w_a), bd(lru_w_x)], axis=-1).astype(BF16)
    bg = jnp.concatenate([lru_b_a, lru_b_x], axis=-1)[:, None, :]
    wal = jnp.concatenate([gla_w_alpha, jnp.zeros((n_layers, 128 - GLA_RANK, GLA_QK), F32)],
                          axis=1).astype(BF16)
    pad_dt = lambda v: jnp.pad(v, ((0, 0), (RD_DT0, 128 - RD_DT0 - SSD_HEADS)))[:, None, :]
    e_np = np.zeros((128, SSD_WIDTH), np.float32)
    for hh in range(SSD_HEADS):
        e_np[RD_DT0 + hh, hh * SSD_HEADDIM:(hh + 1) * SSD_HEADDIM] = 1.0
    e_mat = jnp.asarray(e_np, BF16)
    shift_mat = jnp.asarray(
        np.concatenate([np.eye(TS, k=-(CONV_W - 1 - k), dtype=np.float32)
                        for k in range(CONV_W - 1)], axis=0), BF16)
    wout = w_out.astype(BF16)
    rwt = jnp.swapaxes(router_w, 1, 2)
    ustrict = jnp.asarray(np.triu(np.ones((TB, TB), np.float32), 1), BF16)
    lstrict = jnp.asarray(np.tril(np.ones((N_EXPERTS, N_EXPERTS), np.float32), -1), BF16)
    sw1 = shared_w1.astype(BF16)
    sw3 = shared_w3.astype(BF16)
    sw2 = shared_w2.astype(BF16)

    row = lambda v: v[:, None, :]
    p = dict(win=win, lcw=lru_conv_w, lcb=row(lru_conv_b), wg=wg, bg=bg, lam=row(lru_lambda),
             wal=wal, bal=row(gla_b_alpha), gnorm=row(jnp.tile(gla_norm, (1, GLA_HEADS))),
             scw=ssd_conv_w, scb=row(ssd_conv_b), dtb=pad_dt(ssd_dt_bias), alog=pad_dt(ssd_a_log),
             dskip=row(jnp.repeat(ssd_d, SSD_HEADDIM, axis=1)), snorm=row(ssd_norm),
             e=e_mat, shift=shift_mat, wout=wout, lng=row(ln1_g), lnb=row(ln1_b))
    rbias = router_bias[:, :, None]
    ln2g, ln2b = row(ln2_g), row(ln2_b)

    for l in range(n_layers):
        x1 = _mixer(x, mod, p, alpha, l).reshape(t, d)
        xs, pg, nch, lh = _router(x1, mod, rwt, rbias, ustrict, lstrict, blocks_per_batch, l)
        te, src, dst, nt = _tile_plan(nch, n_tiles_max + NBUF - 1)
        ys = _experts(xs, te, src, dst, nt, exp_w1, exp_w3, exp_w2, n_tiles_max, l)
        x = _combine(x1, mod, pg, lh, ys, sw1, sw3, sw2, ln2g, ln2b, blocks_per_batch, alpha,
                     l).reshape(bsz, seq, d)
    return x
```
